```python
import math
import jax, jax.numpy as jnp
from jax import lax
import numpy as np

D_MODEL = 1024
BATCH = 8
SEQ = 2048
DEPTH = 4
DEC_BATCH = 8
DEC_SEQ = 64
PAST_LEN = 1024

CHUNK = 64
Q_BLOCK = 128
EPS = 1e-6
MASK_NEG = -1e30
ROPE_THETA = 10000.0
A_HEADS = 4
A_DH = 64
A_DV = 2 * A_DH
A_QK = A_HEADS * 2 * A_DH
A_V = A_HEADS * A_DV
B_HEADS = 4
B_DK = 128
B_DV = 128
B_K = B_HEADS * B_DK
B_V = B_HEADS * B_DV
C_WIDTH = 512
C_GROUP = 16
C_NGROUPS = C_WIDTH // C_GROUP
C_STATE = 64
N_EXPERTS = 16
N_EXPERT_GROUPS = 4
EXPERTS_PER_GROUP = N_EXPERTS // N_EXPERT_GROUPS
TOP_K = 2
EXPERT_DFF = 512
IN_COLS = 2 * A_QK + A_V + 2 * B_K + 2 * B_V + C_WIDTH + 3 * D_MODEL

kernel_name = 'hybrid_streaming_encoder_step'


def _rmsnorm(x, g):
    xf = x.astype(jnp.float32)
    y = xf * lax.rsqrt(jnp.mean(xf * xf, axis=-1, keepdims=True) + EPS)
    return (y * g.astype(jnp.float32)).astype(x.dtype)


def _split_cols(proj):
    sizes = (A_QK, A_QK, A_V, B_K, B_K, B_V, B_V, C_WIDTH, D_MODEL, D_MODEL, D_MODEL)
    out, start = [], 0
    for s in sizes:
        out.append(proj[..., start:start + s])
        start += s
    return out


def _rope(x, pos):
    half = A_DH // 2
    inv = ROPE_THETA ** (-jnp.arange(0, A_DH, 2, dtype=jnp.float32) / A_DH)
    ang = pos.astype(jnp.float32)[:, None] * inv[None, :]
    cos = jnp.cos(ang)[None, :, None, None, :]
    sin = jnp.sin(ang)[None, :, None, None, :]
    xf = x.astype(jnp.float32)
    x1, x2 = xf[..., :half], xf[..., half:]
    return jnp.concatenate([x1 * cos - x2 * sin, x2 * cos + x1 * sin], axis=-1).astype(x.dtype)


def _diff_attn_block(q, k, v, lam, mask):
    s = jnp.einsum('bqhmd,bkhmd->bhmqk', q, k).astype(jnp.float32) * (A_DH ** -0.5)
    if mask is not None:
        s = jnp.where(mask, s, MASK_NEG)
    p = jax.nn.softmax(s, axis=-1)
    w = p[:, :, 0] - lam * p[:, :, 1]
    return jnp.einsum('bhqk,bkhe->bqhe', w.astype(v.dtype), v)


def _diff_attn_prompt(q, k, v, lam):
    bsz, t = q.shape[0], q.shape[1]
    nb = t // Q_BLOCK
    qb = q.reshape(bsz, nb, Q_BLOCK, A_HEADS, 2, A_DH).transpose(1, 0, 2, 3, 4, 5)
    k_chunk = jnp.arange(t) // CHUNK

    def one_block(args):
        q_blk, i = args
        q_chunk = (i * Q_BLOCK + jnp.arange(Q_BLOCK)) // CHUNK
        mask = k_chunk[None, :] <= q_chunk[:, None]
        return _diff_attn_block(q_blk, k, v, lam, mask)

    o = lax.map(one_block, (qb, jnp.arange(nb)))
    return o.transpose(1, 0, 2, 3, 4).reshape(bsz, t, A_HEADS, A_DV)


def _hgrn_chunkwise(q, k, v, logf, s0):
    bsz, t = q.shape[0], q.shape[1]
    L = min(CHUNK, t)
    n = t // L

    def to_chunks(a):
        return a.reshape(bsz, n, L, a.shape[2], a.shape[3]).transpose(1, 0, 3, 2, 4)

    causal = jnp.tril(jnp.ones((L, L), dtype=bool))[None, None, :, :, None]

    def step(S, xs):
        qc, kc, vc, gc = xs
        b = jnp.cumsum(gc, axis=2)
        o_inter = jnp.einsum('bhtk,bhkv->bhtv', qc * jnp.exp(b), S)
        diff = b[:, :, :, None, :] - b[:, :, None, :, :]
        decay = jnp.where(causal, jnp.exp(jnp.where(causal, diff, 0.0)), 0.0)
        att = jnp.einsum('bhtk,bhsk,bhtsk->bhts', qc, kc, decay)
        o = o_inter + jnp.einsum('bhts,bhsv->bhtv', att, vc)
        b_last = b[:, :, -1:, :]
        S_new = (jnp.exp(b_last[:, :, 0, :])[..., None] * S
                 + jnp.einsum('bhsk,bhsv->bhkv', kc * jnp.exp(b_last - b), vc))
        return S_new, o

    s_fin, o = lax.scan(step, s0, (to_chunks(q), to_chunks(k), to_chunks(v), to_chunks(logf)))
    return o.transpose(1, 0, 3, 2, 4).reshape(bsz, t, q.shape[2], v.shape[3]), s_fin


def _s5(u, h0_re, h0_im, a_re, a_im, b_re, b_im, c_re, c_im, d, log_dt, w_glu):
    bsz, t = u.shape[0], u.shape[1]
    f32 = jnp.float32
    uf = u.astype(f32).reshape(bsz, t, C_NGROUPS, C_GROUP)
    ar, ai = a_re.astype(f32), a_im.astype(f32)
    dt = jnp.exp(log_dt.astype(f32))[:, None]
    mag = jnp.exp(ar * dt)
    abr, abi = mag * jnp.cos(ai * dt), mag * jnp.sin(ai * dt)
    den = ar * ar + ai * ai
    coef_r = ((abr - 1.0) * ar + abi * ai) / den
    coef_i = (abi * ar - (abr - 1.0) * ai) / den
    bu_r = jnp.einsum('btgj,gpj->btgp', uf, b_re.astype(f32))
    bu_i = jnp.einsum('btgj,gpj->btgp', uf, b_im.astype(f32))
    xr = coef_r * bu_r - coef_i * bu_i
    xi = coef_r * bu_i + coef_i * bu_r
    h0r, h0i = h0_re.astype(f32), h0_im.astype(f32)
    xr = xr.at[:, 0].add(abr * h0r - abi * h0i)
    xi = xi.at[:, 0].add(abr * h0i + abi * h0r)

    def combine(e1, e2):
        a1r, a1i, b1r, b1i = e1
        a2r, a2i, b2r, b2i = e2
        return (a2r * a1r - a2i * a1i, a2r * a1i + a2i * a1r,
                a2r * b1r - a2i * b1i + b2r, a2r * b1i + a2i * b1r + b2i)

    shp = xr.shape
    _, _, sr, si = lax.associative_scan(
        combine, (jnp.broadcast_to(abr, shp), jnp.broadcast_to(abi, shp), xr, xi), axis=1)
    y = (jnp.einsum('gjp,btgp->btgj', c_re.astype(f32), sr)
         - jnp.einsum('gjp,btgp->btgj', c_im.astype(f32), si))
    y = y.reshape(bsz, t, C_WIDTH) + d.astype(f32) * uf.reshape(bsz, t, C_WIDTH)
    z = y.astype(u.dtype) @ w_glu
    out = z[..., :C_WIDTH] * jax.nn.sigmoid(z[..., C_WIDTH:])
    return out, sr[:, -1], si[:, -1]


def _token_mixer(h, pos, l, lbs, p, past_k, past_v, s_h0, s_re0, s_im0):
    bsz, t, _ = h.shape
    aq, ak, av, bq, bf, bi, bg, cu, ga, gb, gc = _split_cols(h @ p['w_in'][l])

    q = _rope(aq.reshape(bsz, t, A_HEADS, 2, A_DH), pos)
    k = _rope(ak.reshape(bsz, t, A_HEADS, 2, A_DH), pos)
    v = av.reshape(bsz, t, A_HEADS, A_DV)
    lam_init = 0.8 - 0.6 * math.exp(-0.3 * l)
    lp = p['diff_lambda'][l].astype(jnp.float32)
    lam = jnp.exp(jnp.sum(lp[0] * lp[1])) - jnp.exp(jnp.sum(lp[2] * lp[3])) + lam_init
    if past_k is None:
        o = _diff_attn_prompt(q, k, v, lam)
    else:
        kk = jnp.concatenate([past_k.reshape(bsz, past_k.shape[1], A_HEADS, 2, A_DH), k], axis=1)
        vv = jnp.concatenate([past_v, v], axis=1)
        o = _diff_attn_block(q, kk, vv, lam, None)
    ya = (_rmsnorm(o, p['diff_subln_g'][l]) * (1.0 - lam_init)).reshape(bsz, t, A_V)

    lb = lbs[l].reshape(B_HEADS, B_DK)
    zf = bf.astype(jnp.float32).reshape(bsz, t, B_HEADS, B_DK)
    f = lb + (1.0 - lb) * jax.nn.sigmoid(zf)
    logf = jnp.log(f)
    kf = 1.0 - f
    qh = jax.nn.silu(bq.astype(jnp.float32)).reshape(bsz, t, B_HEADS, B_DK)
    vh = bi.astype(jnp.float32).reshape(bsz, t, B_HEADS, B_DV)
    oh, s_h = _hgrn_chunkwise(qh, kf, vh, logf, s_h0.astype(jnp.float32))
    yb = (_rmsnorm(oh, p['hgrn_norm_g'][l])
          * jax.nn.silu(bg.astype(jnp.float32)).reshape(bsz, t, B_HEADS, B_DV))
    yb = yb.reshape(bsz, t, B_V).astype(h.dtype)

    yc, s_re, s_im = _s5(cu, s_re0, s_im0, p['s5_a_re'][l], p['s5_a_im'][l], p['s5_b_re'][l],
                         p['s5_b_im'][l], p['s5_c_re'][l], p['s5_c_im'][l], p['s5_d'][l],
                         p['s5_log_dt'][l], p['s5_w_glu'][l])

    merged = (jax.nn.sigmoid(ga) * (ya @ p['w_branch_a'][l])
              + jax.nn.sigmoid(gb) * (yb @ p['w_branch_b'][l])
              + jax.nn.sigmoid(gc) * (yc @ p['w_branch_c'][l]))
    return (merged @ p['w_out'][l], k.reshape(bsz, t, A_HEADS, 2 * A_DH), v, s_h, s_re, s_im)


def _moe(h, router_w, router_b, w_gate, w_up, w_down):
    bsz, t, d = h.shape
    x = h.reshape(bsz * t, d)
    s = jax.nn.softmax((x @ router_w).astype(jnp.float32), axis=-1)
    sb = s + router_b.astype(jnp.float32)
    grp_score = lax.top_k(sb.reshape(-1, N_EXPERT_GROUPS, EXPERTS_PER_GROUP), TOP_K)[0].sum(-1)
    g_sel = jnp.argmax(grp_score, axis=-1)
    in_grp = (jnp.arange(N_EXPERTS) // EXPERTS_PER_GROUP)[None, :] == g_sel[:, None]
    _, idx = lax.top_k(jnp.where(in_grp, sb, -1e9), TOP_K)
    w = jnp.take_along_axis(s, idx, axis=-1)
    w = w / jnp.sum(w, axis=-1, keepdims=True)
    gates = jnp.sum(jax.nn.one_hot(idx, N_EXPERTS, dtype=jnp.float32) * w[..., None], axis=1)
    hid = jax.nn.silu(jnp.einsum('nd,edf->enf', x, w_gate)) * jnp.einsum('nd,edf->enf', x, w_up)
    hid = hid * gates.T.astype(hid.dtype)[:, :, None]
    return jnp.einsum('enf,efd->nd', hid, w_down).reshape(bsz, t, d)


def _trunk(x, c, pos, p, past_k, past_v, s_h, s_re, s_im):
    bsz = x.shape[0]
    lb_p = jax.nn.softmax(p['hgrn_lb'].astype(jnp.float32), axis=0)
    lbs = jnp.cumsum(lb_p, axis=0) - lb_p[0:1]
    c_act = jax.nn.silu(c)
    ks, vs, hs, rs, ims = [], [], [], [], []
    for l in range(DEPTH):
        mod = c_act @ p['w_mod'][l] + p['b_mod'][l]
        sh1, sc1, g1, sh2, sc2, g2 = [m[:, None, :] for m in jnp.split(mod, 6, axis=-1)]
        h = _rmsnorm(x, p['norm1_g'][l]) * (1 + sc1) + sh1
        if past_k is None:
            pk, pv = None, None
            h0 = jnp.zeros((bsz, B_HEADS, B_DK, B_DV), jnp.float32)
            r0 = jnp.zeros((bsz, C_NGROUPS, C_STATE), jnp.float32)
            i0 = r0
        else:
            pk, pv, h0, r0, i0 = past_k[l], past_v[l], s_h[l], s_re[l], s_im[l]
        mix, k_new, v_new, h_new, r_new, i_new = _token_mixer(h, pos, l, lbs, p, pk, pv, h0, r0, i0)
        x = x + g1 * mix
        h2 = _rmsnorm(x, p['norm2_g'][l]) * (1 + sc2) + sh2
        x = x + g2 * _moe(h2, p['router_w'], p['router_b'], p['moe_w_gate'][l],
                          p['moe_w_up'][l], p['moe_w_down'][l])
        ks.append(k_new)
        vs.append(v_new)
        hs.append(h_new)
        rs.append(r_new)
        ims.append(i_new)
    y = _rmsnorm(x, p['final_norm_g'])
    return (y, jnp.stack(ks), jnp.stack(vs), jnp.stack(hs), jnp.stack(rs), jnp.stack(ims))


def setup_inputs(seed: int = 0) -> dict:
    key = jax.random.key(seed)
    ks = jax.random.split(key, 40)
    f32 = jnp.float32

    def nrm(i, shape, scale):
        return jax.random.normal(ks[i], shape, f32) * scale

    a_im_base = jnp.pi * jnp.arange(C_STATE, dtype=f32)
    return {
        'x_prompt': nrm(0, (BATCH, SEQ, D_MODEL), 1.0),
        'x_sample': nrm(1, (DEC_BATCH, DEC_SEQ, D_MODEL), 1.0),
        'cache_k': nrm(2, (DEPTH, DEC_BATCH, PAST_LEN, A_HEADS, 2 * A_DH), 1.0),
        'cache_v': nrm(3, (DEPTH, DEC_BATCH, PAST_LEN, A_HEADS, A_DV), 1.0),
        'state_hgrn': nrm(4, (DEPTH, DEC_BATCH, B_HEADS, B_DK, B_DV), 0.5),
        'state_s5_re': nrm(5, (DEPTH, DEC_BATCH, C_NGROUPS, C_STATE), 0.5),
        'state_s5_im': nrm(6, (DEPTH, DEC_BATCH, C_NGROUPS, C_STATE), 0.5),
        'c_prompt': nrm(7, (BATCH, D_MODEL), 1.0),
        'c_sample': nrm(8, (DEC_BATCH, D_MODEL), 1.0),
        'w_mod': nrm(9, (DEPTH, D_MODEL, 6 * D_MODEL), 0.5 * D_MODEL ** -0.5),
        'b_mod': nrm(10, (DEPTH, 6 * D_MODEL), 0.01),
        'norm1_g': 1.0 + nrm(11, (DEPTH, D_MODEL), 0.01),
        'norm2_g': 1.0 + nrm(12, (DEPTH, D_MODEL), 0.01),
        'w_in': nrm(13, (DEPTH, D_MODEL, IN_COLS), D_MODEL ** -0.5),
        'diff_lambda': nrm(14, (DEPTH, 4, A_DH), 0.1),
        'diff_subln_g': 1.0 + nrm(15, (DEPTH, A_DV), 0.01),
        'hgrn_lb': nrm(16, (DEPTH, B_K), 1.0),
        'hgrn_norm_g': 1.0 + nrm(17, (DEPTH, B_DV), 0.01),
        's5_a_re': -0.5 * jnp.exp(nrm(18, (DEPTH, C_NGROUPS, C_STATE), 0.05)),
        's5_a_im': a_im_base + nrm(19, (DEPTH, C_NGROUPS, C_STATE), 0.01),
        's5_b_re': nrm(20, (DEPTH, C_NGROUPS, C_STATE, C_GROUP), (2 * C_GROUP) ** -0.5),
        's5_b_im': nrm(21, (DEPTH, C_NGROUPS, C_STATE, C_GROUP), (2 * C_GROUP) ** -0.5),
        's5_c_re': nrm(22, (DEPTH, C_NGROUPS, C_GROUP, C_STATE), C_STATE ** -0.5),
        's5_c_im': nrm(23, (DEPTH, C_NGROUPS, C_GROUP, C_STATE), C_STATE ** -0.5),
        's5_d': nrm(24, (DEPTH, C_WIDTH), 1.0),
        's5_log_dt': jax.random.uniform(ks[25], (DEPTH, C_NGROUPS), f32, math.log(1e-3), math.log(1e-1)),
        's5_w_glu': nrm(26, (DEPTH, C_WIDTH, 2 * C_WIDTH), C_WIDTH ** -0.5),
        'w_branch_a': nrm(27, (DEPTH, A_V, D_MODEL), A_V ** -0.5),
        'w_branch_b': nrm(28, (DEPTH, B_V, D_MODEL), B_V ** -0.5),
        'w_branch_c': nrm(29, (DEPTH, C_WIDTH, D_MODEL), C_WIDTH ** -0.5),
        'w_out': nrm(30, (DEPTH, D_MODEL, D_MODEL), D_MODEL ** -0.5),
        'router_w': nrm(31, (D_MODEL, N_EXPERTS), D_MODEL ** -0.5),
        'router_b': nrm(32, (N_EXPERTS,), 0.01),
        'moe_w_gate': nrm(33, (DEPTH, N_EXPERTS, D_MODEL, EXPERT_DFF), D_MODEL ** -0.5),
        'moe_w_up': nrm(34, (DEPTH, N_EXPERTS, D_MODEL, EXPERT_DFF), D_MODEL ** -0.5),
        'moe_w_down': nrm(35, (DEPTH, N_EXPERTS, EXPERT_DFF, D_MODEL), EXPERT_DFF ** -0.5),
        'final_norm_g': 1.0 + nrm(36, (D_MODEL,), 0.01),
    }


def reference(x_prompt, x_sample, cache_k, cache_v, state_hgrn, state_s5_re, state_s5_im,
              c_prompt, c_sample, w_mod, b_mod, norm1_g, norm2_g, w_in, diff_lambda, diff_subln_g,
              hgrn_lb, hgrn_norm_g, s5_a_re, s5_a_im, s5_b_re, s5_b_im, s5_c_re, s5_c_im, s5_d,
              s5_log_dt, s5_w_glu, w_branch_a, w_branch_b, w_branch_c, w_out, router_w, router_b,
              moe_w_gate, moe_w_up, moe_w_down, final_norm_g):
    p = {
        'w_mod': w_mod, 'b_mod': b_mod, 'norm1_g': norm1_g, 'norm2_g': norm2_g, 'w_in': w_in,
        'diff_lambda': diff_lambda, 'diff_subln_g': diff_subln_g, 'hgrn_lb': hgrn_lb,
        'hgrn_norm_g': hgrn_norm_g, 's5_a_re': s5_a_re, 's5_a_im': s5_a_im, 's5_b_re': s5_b_re,
        's5_b_im': s5_b_im, 's5_c_re': s5_c_re, 's5_c_im': s5_c_im, 's5_d': s5_d,
        's5_log_dt': s5_log_dt, 's5_w_glu': s5_w_glu, 'w_branch_a': w_branch_a,
        'w_branch_b': w_branch_b, 'w_branch_c': w_branch_c, 'w_out': w_out,
        'router_w': router_w, 'router_b': router_b, 'moe_w_gate': moe_w_gate,
        'moe_w_up': moe_w_up, 'moe_w_down': moe_w_down, 'final_norm_g': final_norm_g,
    }
    pos_p = jnp.arange(x_prompt.shape[1])
    pos_s = cache_k.shape[2] + jnp.arange(x_sample.shape[1])
    y_prompt, k_p, v_p, h_p, re_p, im_p = _trunk(x_prompt, c_prompt, pos_p, p,
                                                 None, None, None, None, None)
    y_sample, k_s, v_s, h_s, re_s, im_s = _trunk(x_sample, c_sample, pos_s, p, cache_k, cache_v,
                                                 state_hgrn, state_s5_re, state_s5_im)
    return (y_prompt, y_sample, k_p, v_p, h_p, re_p, im_p, k_s, v_s, h_s, re_s, im_s)
```

```python
import functools
import math

import jax
import jax.numpy as jnp
from jax import lax
from jax.experimental import pallas as pl
from jax.experimental.pallas import tpu as pltpu

F32 = jnp.float32
BF16 = jnp.bfloat16

D_MODEL = 1024
CHUNK = 64
EPS = 1e-6
MASK_NEG = -1e30
ROPE_THETA = 10000.0
A_HEADS = 4
A_DH = 64
A_DV = 128
A_QK = 512
B_HEADS = 4
B_DK = 128
C_WIDTH = 512
C_GROUP = 16
C_NGROUPS = 32
C_STATE = 64
C_LANES = C_NGROUPS * C_STATE
N_EXPERTS = 16
N_EXPERT_GROUPS = 4
EXPERTS_PER_GROUP = 4
EXPERT_DFF = 512
IN_COLS = 7168
COL_AK, COL_AV, COL_B, COL_CU, COL_G = 512, 1024, 1536, 3584, 4096

LANE = 128
VMEM_LIMIT = 56 * 1024 * 1024


def _cparams(sem):
    return pltpu.CompilerParams(dimension_semantics=sem, vmem_limit_bytes=VMEM_LIMIT)


def _sigmoid(x):
    return jax.nn.sigmoid(x)


def _dot(a, b):
    return jnp.dot(a, b, preferred_element_type=F32)


def _dot_nt(a, b):
    return lax.dot_general(a, b, (((1,), (1,)), ((), ())), preferred_element_type=F32)


def _dot_tn(a, b):
    return lax.dot_general(a, b, (((0,), (0,)), ((), ())), preferred_element_type=F32)


def _mod_kernel(c_ref, w_ref, b_ref, o_ref):
    c = c_ref[...]
    a = (c * _sigmoid(c)).astype(BF16)
    o_ref[...] = _dot(a, w_ref[...].astype(BF16)) + b_ref[...]


def _modulation(c_all, w_mod, b_mod):
    depth, d, n6 = w_mod.shape
    nb = c_all.shape[0]
    tn = 1536
    return pl.pallas_call(
        _mod_kernel,
        grid=(depth, n6 // tn),
        in_specs=[
            pl.BlockSpec((nb, d), lambda l, j: (0, 0)),
            pl.BlockSpec((None, d, tn), lambda l, j: (l, 0, j)),
            pl.BlockSpec((None, 1, tn), lambda l, j: (l, 0, j)),
        ],
        out_specs=pl.BlockSpec((None, nb, tn), lambda l, j: (l, 0, j)),
        out_shape=jax.ShapeDtypeStruct((depth, nb, n6), F32),
        compiler_params=_cparams(("parallel", "parallel")),
        name="modulation",
    )(c_all, w_mod, b_mod.reshape(depth, 1, n6))


def _swap_half(a):
    parts = []
    for j in range(a.shape[1] // LANE):
        s = a[:, LANE * j:LANE * (j + 1)]
        lane = lax.broadcasted_iota(jnp.int32, s.shape, 1)
        first = (lane & (A_DH - 1)) < (A_DH // 2)
        parts.append(jnp.where(first, pltpu.roll(s, LANE - A_DH // 2, 1), pltpu.roll(s, A_DH // 2, 1)))
    return jnp.concatenate(parts, axis=1)


def _in_kernel(x_ref, mod_ref, g_ref, w_ref, cos_ref, sin_ref,
               q_ref, k_ref, v_ref, b_ref, cu_ref, gt_ref):
    x = x_ref[...]
    ms = jnp.mean(x * x, axis=-1, keepdims=True)
    y = x * lax.rsqrt(ms + EPS) * g_ref[...]
    h = y * (1.0 + mod_ref[:, D_MODEL:2 * D_MODEL]) + mod_ref[:, 0:D_MODEL]
    hb = h.astype(BF16)

    def proj(c0, width):
        return _dot(hb, w_ref[:, c0:c0 + width])

    cos = cos_ref[...]
    sin = sin_ref[...]

    def rope(a):
        return a * cos + _swap_half(a) * sin

    q_ref[...] = (rope(proj(0, A_QK)) * (A_DH ** -0.5)).astype(BF16)
    k_ref[...] = rope(proj(COL_AK, A_QK))
    v_ref[...] = proj(COL_AV, 512)
    for j in range(4):
        b_ref[:, 512 * j:512 * (j + 1)] = proj(COL_B + 512 * j, 512)
    cu_ref[...] = proj(COL_CU, C_WIDTH)
    for j in range(6):
        gt_ref[:, 512 * j:512 * (j + 1)] = proj(COL_G + 512 * j, 512)


def _in_projection(x, mod, g1, w_in, cos, sin, tm):
    bsz, t, d = x.shape
    nt = t // tm
    row = lambda b, i: (b, i, 0)
    return pl.pallas_call(
        _in_kernel,
        grid=(bsz, nt),
        in_specs=[
            pl.BlockSpec((None, tm, d), row),
            pl.BlockSpec((None, 1, 6 * d), lambda b, i: (b, 0, 0)),
            pl.BlockSpec((1, d), lambda b, i: (0, 0)),
            pl.BlockSpec((d, IN_COLS), lambda b, i: (0, 0), pipeline_mode=pl.Buffered(1)),
            pl.BlockSpec((tm, A_QK), lambda b, i: (i, 0)),
            pl.BlockSpec((tm, A_QK), lambda b, i: (i, 0)),
        ],
        out_specs=[
            pl.BlockSpec((None, tm, A_QK), row),
            pl.BlockSpec((None, tm, A_QK), row),
            pl.BlockSpec((None, tm, 512), row),
            pl.BlockSpec((None, tm, 2048), row),
            pl.BlockSpec((tm, C_WIDTH), lambda b, i: (i, b)),
            pl.BlockSpec((None, tm, 3 * d), row),
        ],
        out_shape=[
            jax.ShapeDtypeStruct((bsz, t, A_QK), BF16),
            jax.ShapeDtypeStruct((bsz, t, A_QK), F32),
            jax.ShapeDtypeStruct((bsz, t, 512), F32),
            jax.ShapeDtypeStruct((bsz, t, 2048), F32),
            jax.ShapeDtypeStruct((t, bsz * C_WIDTH), F32),
            jax.ShapeDtypeStruct((bsz, t, 3 * d), F32),
        ],
        compiler_params=_cparams(("parallel", "parallel")),
        name="in_projection",
    )(x, mod, g1, w_in, cos, sin)


def _split_maps(q):
    lane = lax.broadcasted_iota(jnp.int32, q.shape, 1)
    zero = jnp.zeros_like(q)
    return jnp.concatenate([jnp.where(lane < A_DH, q, zero), jnp.where(lane >= A_DH, q, zero)], axis=0)


def _attn_finish(acc, l, lam, g, scale_out, tq):
    o = acc[:tq] / l[:tq] - lam * (acc[tq:] / l[tq:])
    ms = jnp.mean(o * o, axis=-1, keepdims=True)
    return (o * lax.rsqrt(ms + EPS) * g) * scale_out


def _attn_prompt_kernel(lam_ref, q_ref, k_ref, v_ref, g_ref, o_ref, kb, vb, *, tq, scale_out):
    i = pl.program_id(2)

    @pl.when(i == 0)
    def _():
        kb[...] = k_ref[...].astype(BF16)
        vb[...] = v_ref[...].astype(BF16)

    qq = _split_maps(q_ref[...])
    row = lax.broadcasted_iota(jnp.int32, (2 * tq, tq), 0)
    col = lax.broadcasted_iota(jnp.int32, (2 * tq, tq), 1)
    q_chunk = (i * tq + (row & (tq - 1))) >> 6

    def body(j, carry):
        m, l, acc = carry
        r0 = pl.multiple_of(j * tq, tq)
        ks = kb[pl.ds(r0, tq), :]
        vs = vb[pl.ds(r0, tq), :]
        s = _dot_nt(qq, ks)
        s = jnp.where(((j * tq + col) >> 6) <= q_chunk, s, MASK_NEG)
        mn = jnp.maximum(m, jnp.max(s, axis=-1, keepdims=True))
        a = jnp.exp(m - mn)
        p = jnp.exp(s - mn)
        l = a * l + jnp.sum(p, axis=-1, keepdims=True)
        acc = a * acc + _dot(p.astype(BF16), vs)
        return mn, l, acc

    m0 = jnp.full((2 * tq, 1), MASK_NEG, F32)
    l0 = jnp.zeros((2 * tq, 1), F32)
    a0 = jnp.zeros((2 * tq, A_DV), F32)
    _, l, acc = lax.fori_loop(0, i + 1, body, (m0, l0, a0))
    o_ref[...] = _attn_finish(acc, l, lam_ref[...], g_ref[...], scale_out, tq).astype(BF16)


def _attn_prompt(lam, q, k, v, g, scale_out, tq):
    bsz, t, _ = q.shape
    assert tq % CHUNK == 0 and tq & (tq - 1) == 0
    kern = functools.partial(_attn_prompt_kernel, tq=tq, scale_out=scale_out)
    return pl.pallas_call(
        kern,
        grid=(bsz, A_HEADS, t // tq),
        in_specs=[
            pl.BlockSpec((1, LANE), lambda b, h, i: (0, 0)),
            pl.BlockSpec((None, tq, LANE), lambda b, h, i: (b, i, h)),
            pl.BlockSpec((None, t, LANE), lambda b, h, i: (b, 0, h)),
            pl.BlockSpec((None, t, LANE), lambda b, h, i: (b, 0, h)),
            pl.BlockSpec((1, LANE), lambda b, h, i: (0, 0)),
        ],
        out_specs=pl.BlockSpec((None, tq, LANE), lambda b, h, i: (b, i, h)),
        out_shape=jax.ShapeDtypeStruct((bsz, t, A_HEADS * A_DV), BF16),
        scratch_shapes=[pltpu.VMEM((t, LANE), BF16), pltpu.VMEM((t, LANE), BF16)],
        compiler_params=_cparams(("parallel", "parallel", "arbitrary")),
        name="attn_prompt",
    )(lam, q, k, v, g)


def _attn_sample_kernel(lam_ref, q_ref, pk_ref, pv_ref, k_ref, v_ref, g_ref, o_ref, *, tq, scale_out):
    qq = _split_maps(q_ref[...])
    s_p = _dot_nt(qq, pk_ref[...].astype(BF16))
    s_n = _dot_nt(qq, k_ref[...].astype(BF16))
    m = jnp.maximum(jnp.max(s_p, axis=-1, keepdims=True), jnp.max(s_n, axis=-1, keepdims=True))
    p_p = jnp.exp(s_p - m)
    p_n = jnp.exp(s_n - m)
    l = jnp.sum(p_p, axis=-1, keepdims=True) + jnp.sum(p_n, axis=-1, keepdims=True)
    acc = _dot(p_p.astype(BF16), pv_ref[...].astype(BF16)) + _dot(p_n.astype(BF16), v_ref[...].astype(BF16))
    o_ref[...] = _attn_finish(acc, l, lam_ref[...], g_ref[...], scale_out, tq).astype(BF16)


def _attn_sample(lam, q, cache_k, cache_v, layer, k, v, g, scale_out):
    bsz, t, _ = q.shape
    past = cache_k.shape[2]
    kern = functools.partial(_attn_sample_kernel, tq=t, scale_out=scale_out)
    new = lambda b, h: (b, 0, h)
    old = lambda b, h: (layer, b, 0, h)
    return pl.pallas_call(
        kern,
        grid=(bsz, A_HEADS),
        in_specs=[
            pl.BlockSpec((1, LANE), lambda b, h: (0, 0)),
            pl.BlockSpec((None, t, LANE), new),
            pl.BlockSpec((None, None, past, LANE), old),
            pl.BlockSpec((None, None, past, LANE), old),
            pl.BlockSpec((None, t, LANE), new),
            pl.BlockSpec((None, t, LANE), new),
            pl.BlockSpec((1, LANE), lambda b, h: (0, 0)),
        ],
        out_specs=pl.BlockSpec((None, t, LANE), new),
        out_shape=jax.ShapeDtypeStruct((bsz, t, A_HEADS * A_DV), BF16),
        compiler_params=_cparams(("parallel", "parallel")),
        name="attn_sample",
    )(lam, q, cache_k, cache_v, k, v, g)


def _split3(x):
    hi = x.astype(BF16)
    r = x - hi.astype(F32)
    mid = r.astype(BF16)
    lo = (r - mid.astype(F32)).astype(BF16)
    return hi, mid, lo


def _hgrn_chunk(qraw, fraw, v, graw, lb, gn, st):
    L = CHUNK
    f = lb + (1.0 - lb) * _sigmoid(fraw)
    logf = jnp.log(f)
    kf = 1.0 - f
    q = qraw * _sigmoid(qraw)

    row = lax.broadcasted_iota(jnp.int32, (L, B_DK), 0)
    r64 = lax.broadcasted_iota(jnp.int32, (L, L), 0)
    c64 = lax.broadcasted_iota(jnp.int32, (L, L), 1)

    tril = jnp.where(c64 <= r64, 1.0, 0.0).astype(BF16)
    hi, mid, lo = _split3(logf)
    b = _dot(jnp.concatenate([tril, tril, tril], axis=1), jnp.concatenate([hi, mid, lo], axis=0))

    vb = v.astype(BF16)
    o = _dot_nt((q * jnp.exp(b)).astype(BF16), st.astype(BF16))

    att = jnp.zeros((L, L), F32)
    for sh in (5, 4, 3):
        m = 1 << sh
        ref_rows = [jnp.broadcast_to(b[(2 * j + 1) * m - 1:(2 * j + 1) * m, :], (2 * m, B_DK))
                    for j in range(L // (2 * m))]
        ref = ref_rows[0] if len(ref_rows) == 1 else jnp.concatenate(ref_rows, axis=0)
        w = jnp.exp(-jnp.abs(b - ref))
        upper = ((row >> sh) & 1) == 1
        qt = jnp.where(upper, q * w, 0.0).astype(BF16)
        kt = jnp.where(upper, 0.0, kf * w).astype(BF16)
        pair = (((r64 >> sh) & 1) == 1) & ((r64 >> (sh + 1)) == (c64 >> (sh + 1))) & (((c64 >> sh) & 1) == 0)
        att = att + jnp.where(pair, _dot_nt(qt, kt), 0.0)
    for d in range(8):
        if d == 0:
            e = q * kf
        else:
            ok = (row & 7) >= d
            e = q * pltpu.roll(kf, d, 0) * jnp.exp(jnp.where(ok, b - pltpu.roll(b, d, 0), 0.0))
        rs = jnp.sum(e, axis=-1, keepdims=True)
        att = att + jnp.where((c64 == r64 - d) & ((r64 & 7) >= d), rs, 0.0)
    o = o + _dot(att.astype(BF16), vb)

    b_last = b[L - 1:L, :]
    kdec = (kf * jnp.exp(b_last - b)).astype(BF16)
    st_new = st * jnp.exp(b_last) + _dot_tn(vb, kdec)

    ms = jnp.mean(o * o, axis=-1, keepdims=True)
    y = o * lax.rsqrt(ms + EPS) * gn * (graw * _sigmoid(graw))
    return y, st_new


def _hgrn_kernel(*refs, nchunk, has_s0):
    if has_s0:
        q_ref, f_ref, v_ref, g_ref, lb_ref, gn_ref, s0_ref, y_ref, sf_ref, st = refs
    else:
        q_ref, f_ref, v_ref, g_ref, lb_ref, gn_ref, y_ref, sf_ref, st = refs
    c = pl.program_id(2)

    @pl.when(c == 0)
    def _():
        if has_s0:
            st[...] = s0_ref[...].T
        else:
            st[...] = jnp.zeros_like(st)

    lb = lb_ref[...]
    gn = gn_ref[...]

    def body(n, carry):
        r0 = pl.multiple_of(n * CHUNK, CHUNK)
        sl = pl.ds(r0, CHUNK)
        y, st_new = _hgrn_chunk(q_ref[sl, :], f_ref[sl, :], v_ref[sl, :], g_ref[sl, :], lb, gn, st[...])
        st[...] = st_new
        y_ref[sl, :] = y.astype(BF16)
        return carry

    lax.fori_loop(0, nchunk, body, 0)

    @pl.when(c == pl.num_programs(2) - 1)
    def _():
        sf_ref[...] = st[...].T


def _hgrn(braw, lb, gn, s0, layer, tb):
    bsz, t, _ = braw.shape
    has_s0 = s0 is not None
    kern = functools.partial(_hgrn_kernel, nchunk=tb // CHUNK, has_s0=has_s0)

    def col(off):
        return pl.BlockSpec((None, tb, LANE), lambda b, h, c: (b, c, off + h))

    in_specs = [col(0), col(4), col(8), col(12),
                pl.BlockSpec((1, LANE), lambda b, h, c: (0, h)),
                pl.BlockSpec((1, LANE), lambda b, h, c: (0, 0))]
    args = [braw, braw, braw, braw, lb, gn]
    if has_s0:
        in_specs.append(pl.BlockSpec((None, None, None, B_DK, LANE), lambda b, h, c: (layer, b, h, 0, 0)))
        args.append(s0)
    return pl.pallas_call(
        kern,
        grid=(bsz, B_HEADS, t // tb),
        in_specs=in_specs,
        out_specs=[
            pl.BlockSpec((None, tb, LANE), lambda b, h, c: (b, c, h)),
            pl.BlockSpec((None, None, B_DK, LANE), lambda b, h, c: (b, h, 0, 0)),
        ],
        out_shape=[
            jax.ShapeDtypeStruct((bsz, t, B_HEADS * LANE), BF16),
            jax.ShapeDtypeStruct((bsz, B_HEADS, B_DK, LANE), F32),
        ],
        scratch_shapes=[pltpu.VMEM((LANE, B_DK), F32)],
        compiler_params=_cparams(("parallel", "parallel", "arbitrary")),
        name="hgrn",
    )(*args)


S5_COLS = 512


def _s5_kernel(u_ref, bm_ref, cm_ref, a_ref, d_ref, h0_ref, y_ref, hf_ref, xs, hs, *, tc, nb):
    step_id = pl.program_id(0)

    @pl.when(step_id == 0)
    def _():
        hs[...] = h0_ref[...]

    u = u_ref[...]
    xs[...] = _dot(u.astype(BF16), bm_ref[...])
    for cc in range(C_LANES // S5_COLS):
        re = slice(cc * S5_COLS, (cc + 1) * S5_COLS)
        im = slice(C_LANES + cc * S5_COLS, C_LANES + (cc + 1) * S5_COLS)
        ar = a_ref[0:nb, re]
        ai = a_ref[nb:2 * nb, re]

        def step(t, carry):
            hr, hi = carry
            rows = pl.ds(pl.multiple_of(t * nb, nb), nb)
            nr = ar * hr - ai * hi + xs[rows, re]
            ni = ar * hi + ai * hr + xs[rows, im]
            xs[rows, re] = nr
            xs[rows, im] = ni
            return nr, ni

        hr, hi = lax.fori_loop(0, tc, step, (hs[:, re], hs[:, im]), unroll=8)
        hs[:, re] = hr
        hs[:, im] = hi
    y = _dot(xs[...].astype(BF16), cm_ref[...]) + d_ref[...] * u
    y_ref[...] = y.astype(BF16)
    hf_ref[...] = hs[...]


def _s5(u_tm, bm, cm, a, d, h0, tc):
    n, _ = u_tm.shape
    nb = h0.shape[0]
    rows = tc * nb
    kern = functools.partial(_s5_kernel, tc=tc, nb=nb)
    const = lambda i: (0, 0)
    return pl.pallas_call(
        kern,
        grid=(n // rows,),
        in_specs=[
            pl.BlockSpec((rows, C_WIDTH), lambda i: (i, 0)),
            pl.BlockSpec((C_WIDTH, 2 * C_LANES), const, pipeline_mode=pl.Buffered(1)),
            pl.BlockSpec((2 * C_LANES, C_WIDTH), const, pipeline_mode=pl.Buffered(1)),
            pl.BlockSpec((2 * nb, C_LANES), const),
            pl.BlockSpec((1, C_WIDTH), const),
            pl.BlockSpec((nb, 2 * C_LANES), const),
        ],
        out_specs=[
            pl.BlockSpec((rows, C_WIDTH), lambda i: (i, 0)),
            pl.BlockSpec((nb, 2 * C_LANES), const),
        ],
        out_shape=[
            jax.ShapeDtypeStruct((n, C_WIDTH), BF16),
            jax.ShapeDtypeStruct((nb, 2 * C_LANES), F32),
        ],
        scratch_shapes=[pltpu.VMEM((rows, 2 * C_LANES), F32), pltpu.VMEM((nb, 2 * C_LANES), F32)],
        compiler_params=_cparams(("arbitrary",)),
        name="s5",
    )(u_tm, bm, cm, a, d, h0)


def _s5_tables(a_re, a_im, b_re, b_im, c_re, c_im, log_dt, nb):
    dt = jnp.exp(log_dt)[:, None]
    mag = jnp.exp(a_re * dt)
    abr, abi = mag * jnp.cos(a_im * dt), mag * jnp.sin(a_im * dt)
    den = a_re * a_re + a_im * a_im
    coef_r = ((abr - 1.0) * a_re + abi * a_im) / den
    coef_i = (abi * a_re - (abr - 1.0) * a_im) / den
    fr = coef_r[..., None] * b_re - coef_i[..., None] * b_im
    fi = coef_r[..., None] * b_im + coef_i[..., None] * b_re
    eye = jnp.eye(C_NGROUPS, dtype=F32)

    def blockdiag_in(m):
        return jnp.einsum('gpj,gh->gjhp', m, eye).reshape(C_WIDTH, C_LANES)

    def blockdiag_out(m):
        return jnp.einsum('gjp,gh->gphj', m, eye).reshape(C_LANES, C_WIDTH)

    bm = jnp.concatenate([blockdiag_in(fr), blockdiag_in(fi)], axis=1).astype(BF16)
    cm = jnp.concatenate([blockdiag_out(c_re), -blockdiag_out(c_im)], axis=0).astype(BF16)
    a = jnp.concatenate([jnp.broadcast_to(abr.reshape(1, C_LANES), (nb, C_LANES)),
                         jnp.broadcast_to(abi.reshape(1, C_LANES), (nb, C_LANES))], axis=0)
    return bm, cm, a


def _pair_max_sum(v):
    best = v[0:1] + v[1:2]
    for i, j in ((0, 2), (0, 3), (1, 2), (1, 3), (2, 3)):
        best = jnp.maximum(best, v[i:i + 1] + v[j:j + 1])
    return best


def _route(logits_t, rb):
    mx = jnp.max(logits_t, axis=0, keepdims=True)
    ex = jnp.exp(logits_t - mx)
    s = ex / jnp.sum(ex, axis=0, keepdims=True)
    sb = s + rb
    g = EXPERTS_PER_GROUP
    best = _pair_max_sum(sb[0:g])
    sel = jnp.zeros(best.shape, jnp.int32)
    for gi in range(1, N_EXPERT_GROUPS):
        sc = _pair_max_sum(sb[gi * g:(gi + 1) * g])
        take = sc > best
        best = jnp.where(take, sc, best)
        sel = jnp.where(take, gi, sel)
    sb4 = jnp.zeros((g,) + best.shape[1:], F32)
    s4 = jnp.zeros((g,) + best.shape[1:], F32)
    for gi in range(N_EXPERT_GROUPS):
        on = sel == gi
        sb4 = jnp.where(on, sb[gi * g:(gi + 1) * g], sb4)
        s4 = jnp.where(on, s[gi * g:(gi + 1) * g], s4)
    rows = []
    for e in range(g):
        rank = jnp.zeros(best.shape, jnp.int32)
        for j in range(g):
            if j == e:
                continue
            ahead = (sb4[j:j + 1] > sb4[e:e + 1]) if j > e else (sb4[j:j + 1] >= sb4[e:e + 1])
            rank = rank + ahead.astype(jnp.int32)
        rows.append(jnp.where(rank < 2, s4[e:e + 1], 0.0))
    w4 = jnp.concatenate(rows, axis=0)
    w4 = w4 / jnp.sum(w4, axis=0, keepdims=True)
    return jnp.concatenate([jnp.where(sel == gi, w4, 0.0) for gi in range(N_EXPERT_GROUPS)], axis=0)


def _merge_kernel(x_ref, ya_ref, yb_ref, ys_ref, gt_ref, mod_ref, g2_ref,
                  wglu_ref, wa_ref, wb_ref, wc_ref, wout_ref, rw_ref, rb_ref,
                  x1_ref, h2_ref, gates_ref):
    d = D_MODEL
    z = _dot(ys_ref[...], wglu_ref[...])
    yc = (z[:, :C_WIDTH] * _sigmoid(z[:, C_WIDTH:])).astype(BF16)
    merged = (_sigmoid(gt_ref[:, 0:d]) * _dot(ya_ref[...], wa_ref[...])
              + _sigmoid(gt_ref[:, d:2 * d]) * _dot(yb_ref[...], wb_ref[...])
              + _sigmoid(gt_ref[:, 2 * d:3 * d]) * _dot(yc, wc_ref[...]))
    mix = _dot(merged.astype(BF16), wout_ref[...])
    x1 = x_ref[...] + mod_ref[:, 2 * d:3 * d] * mix
    x1_ref[...] = x1
    ms = jnp.mean(x1 * x1, axis=-1, keepdims=True)
    h2 = (x1 * lax.rsqrt(ms + EPS) * g2_ref[...]) * (1.0 + mod_ref[:, 4 * d:5 * d]) + mod_ref[:, 3 * d:4 * d]
    h_hi, h_mid, h_lo = _split3(h2)
    h2_ref[...] = h_hi
    hcat = jnp.concatenate([h_hi, h_mid, h_hi, h_lo, h_mid, h_hi], axis=1)
    gates_t = _route(_dot_nt(rw_ref[...], hcat), rb_ref[...])
    pad = jnp.zeros((LANE - N_EXPERTS, gates_t.shape[1]), F32)
    gates_ref[...] = jnp.concatenate([gates_t, pad], axis=0).T


def _merge(x, ya, yb, ys_tm, graw, mod, g2, wglu, wa, wb, wc, wout, rw_t, rb, tm):
    bsz, t, d = x.shape
    row = lambda b, i: (b, i, 0)
    const = lambda b, i: (0, 0)

    def wspec(shape):
        return pl.BlockSpec(shape, const, pipeline_mode=pl.Buffered(1))

    return pl.pallas_call(
        _merge_kernel,
        grid=(bsz, t // tm),
        in_specs=[
            pl.BlockSpec((None, tm, d), row),
            pl.BlockSpec((None, tm, 512), row),
            pl.BlockSpec((None, tm, 512), row),
            pl.BlockSpec((tm, C_WIDTH), lambda b, i: (i, b)),
            pl.BlockSpec((None, tm, 3 * d), row),
            pl.BlockSpec((None, 1, 6 * d), lambda b, i: (b, 0, 0)),
            pl.BlockSpec((1, d), const),
            wspec((C_WIDTH, 2 * C_WIDTH)), wspec((512, d)), wspec((512, d)), wspec((C_WIDTH, d)),
            wspec((d, d)), wspec((N_EXPERTS, 6 * d)), wspec((N_EXPERTS, 1)),
        ],
        out_specs=[
            pl.BlockSpec((None, tm, d), row),
            pl.BlockSpec((None, tm, d), row),
            pl.BlockSpec((None, tm, LANE), row),
        ],
        out_shape=[
            jax.ShapeDtypeStruct((bsz, t, d), F32),
            jax.ShapeDtypeStruct((bsz, t, d), BF16),
            jax.ShapeDtypeStruct((bsz, t, LANE), F32),
        ],
        compiler_params=_cparams(("parallel", "parallel")),
        name="merge_router",
    )(x, ya, yb, ys_tm, graw, mod, g2, wglu, wa, wb, wc, wout, rw_t, rb)


def _moe_kernel(h_ref, gates_ref, x1_ref, mod_ref, wg_ref, wu_ref, wd_ref, gf_ref, o_ref, acc, *, final):
    e = pl.program_id(1)
    nbt, tb, d = h_ref.shape
    tm = nbt * tb

    @pl.when(e == 0)
    def _():
        acc[...] = jnp.zeros_like(acc)

    h = h_ref[...].reshape(tm, d)
    a = _dot(h, wg_ref[...].astype(BF16))
    u = _dot(h, wu_ref[...].astype(BF16))
    gates = gates_ref[...].reshape(tm, LANE)
    lane = lax.broadcasted_iota(jnp.int32, gates.shape, 1)
    gcol = jnp.sum(jnp.where(lane == e, gates, 0.0), axis=-1, keepdims=True)
    hid = (a * _sigmoid(a)) * u * gcol
    acc[...] += _dot(hid.astype(BF16), wd_ref[...].astype(BF16))

    @pl.when(e == pl.num_programs(1) - 1)
    def _():
        x2 = x1_ref[...] + mod_ref[:, :, 5 * d:6 * d] * acc[...].reshape(nbt, tb, d)
        if final:
            ms = jnp.mean(x2 * x2, axis=-1, keepdims=True)
            x2 = x2 * lax.rsqrt(ms + EPS) * gf_ref[...]
        o_ref[...] = x2


def _moe(h2, gates, x1, mod, wg, wu, wd, gfinal, layer, nbt, tb, final):
    bsz, t, d = x1.shape
    kern = functools.partial(_moe_kernel, final=final)
    tok = lambda i, e: (i // (t // tb), i % (t // tb), 0) if nbt == 1 else (i, 0, 0)
    nsteps = (bsz // nbt) * (t // tb)
    return pl.pallas_call(
        kern,
        grid=(nsteps, N_EXPERTS),
        in_specs=[
            pl.BlockSpec((nbt, tb, d), tok),
            pl.BlockSpec((nbt, tb, LANE), tok),
            pl.BlockSpec((nbt, tb, d), tok),
            pl.BlockSpec((nbt, 1, 6 * d), lambda i, e: ((i // (t // tb)) if nbt == 1 else i, 0, 0)),
            pl.BlockSpec((None, None, d, EXPERT_DFF), lambda i, e: (layer, e, 0, 0)),
            pl.BlockSpec((None, None, d, EXPERT_DFF), lambda i, e: (layer, e, 0, 0)),
            pl.BlockSpec((None, None, EXPERT_DFF, d), lambda i, e: (layer, e, 0, 0)),
            pl.BlockSpec((1, d), lambda i, e: (0, 0)),
        ],
        out_specs=pl.BlockSpec((nbt, tb, d), tok),
        out_shape=jax.ShapeDtypeStruct((bsz, t, d), F32),
        scratch_shapes=[pltpu.VMEM((nbt * tb, d), F32)],
        compiler_params=_cparams(("parallel", "arbitrary")),
        name="moe",
    )(h2, gates, x1, mod, wg, wu, wd, gfinal)


def _rope_tables(pos):
    inv = ROPE_THETA ** (-jnp.arange(0, A_DH, 2, dtype=F32) / A_DH)
    ang = pos.astype(F32)[:, None] * inv[None, :]
    cos, sin = jnp.cos(ang), jnp.sin(ang)
    seg_cos = jnp.concatenate([cos, cos], axis=1)
    seg_sin = jnp.concatenate([-sin, sin], axis=1)
    reps = A_QK // A_DH
    return jnp.tile(seg_cos, (1, reps)), jnp.tile(seg_sin, (1, reps))


def _trunk(x, mods, pos, p, wts, past, tiles):
    bsz, t, d = x.shape
    depth = mods.shape[0]
    cos, sin = _rope_tables(pos)
    lb_p = jax.nn.softmax(p['hgrn_lb'].astype(F32), axis=0)
    lbs = jnp.cumsum(lb_p, axis=0) - lb_p[0:1]
    w_hi, w_mid, w_lo = _split3(p['router_w'].T.astype(F32))
    rw_t = jnp.concatenate([w_hi, w_hi, w_mid, w_hi, w_mid, w_lo], axis=1)
    rb = p['router_b'].reshape(N_EXPERTS, 1).astype(F32)
    gfinal = p['final_norm_g'].reshape(1, d)
    ks, vs, hs, rs, ims = [], [], [], [], []
    for l in range(depth):
        mod = mods[l]
        q, k, v, braw, cu_tm, graw = _in_projection(
            x, mod, p['norm1_g'][l].reshape(1, d), wts['w_in'][l], cos, sin, tiles['tm'])

        lam_init = 0.8 - 0.6 * math.exp(-0.3 * l)
        lp = p['diff_lambda'][l].astype(F32)
        lam = jnp.exp(jnp.sum(lp[0] * lp[1])) - jnp.exp(jnp.sum(lp[2] * lp[3])) + lam_init
        lam_row = jnp.full((1, LANE), lam, F32)
        subg = p['diff_subln_g'][l].reshape(1, A_DV)
        if past is None:
            ya = _attn_prompt(lam_row, q, k, v, subg, 1.0 - lam_init, tiles['tq'])
        else:
            pk = past[0].reshape(depth, bsz, -1, A_HEADS * 2 * A_DH)
            pv = past[1].reshape(depth, bsz, -1, A_HEADS * A_DV)
            ya = _attn_sample(lam_row, q, pk, pv, l, k, v, subg, 1.0 - lam_init)

        yb, s_h = _hgrn(braw, lbs[l].reshape(1, B_HEADS * B_DK), p['hgrn_norm_g'][l].reshape(1, LANE),
                        None if past is None else past[2], l, tiles['tb'])

        bm, cm, a_tab = _s5_tables(p['s5_a_re'][l], p['s5_a_im'][l], p['s5_b_re'][l], p['s5_b_im'][l],
                                   p['s5_c_re'][l], p['s5_c_im'][l], p['s5_log_dt'][l], bsz)
        if past is None:
            h0 = jnp.zeros((bsz, 2 * C_LANES), F32)
        else:
            h0 = jnp.concatenate([past[3][l].reshape(bsz, C_LANES), past[4][l].reshape(bsz, C_LANES)], axis=1)
        ys_tm, s5_state = _s5(cu_tm.reshape(t * bsz, C_WIDTH), bm, cm, a_tab,
                              p['s5_d'][l].reshape(1, C_WIDTH), h0, tiles['tc'])

        x1, h2, gates = _merge(x, ya, yb, ys_tm.reshape(t, bsz * C_WIDTH), graw, mod,
                               p['norm2_g'][l].reshape(1, d), wts['w_glu'][l], wts['w_a'][l], wts['w_b'][l],
                               wts['w_c'][l], wts['w_out'][l], rw_t, rb, tiles['tm'])
        x = _moe(h2, gates, x1, mod, p['moe_w_gate'], p['moe_w_up'], p['moe_w_down'], gfinal, l,
                 tiles['moe_nb'], tiles['moe_tb'], final=(l == depth - 1))

        ks.append(k.reshape(bsz, t, A_HEADS, 2 * A_DH))
        vs.append(v.reshape(bsz, t, A_HEADS, A_DV))
        hs.append(s_h)
        rs.append(s5_state[:, :C_LANES].reshape(bsz, C_NGROUPS, C_STATE))
        ims.append(s5_state[:, C_LANES:].reshape(bsz, C_NGROUPS, C_STATE))
    return (x, jnp.stack(ks), jnp.stack(vs), jnp.stack(hs), jnp.stack(rs), jnp.stack(ims))


def kernel(x_prompt, x_sample, cache_k, cache_v, state_hgrn, state_s5_re, state_s5_im, c_prompt, c_sample, w_mod, b_mod, norm1_g, norm2_g, w_in, diff_lambda, diff_subln_g, hgrn_lb, hgrn_norm_g, s5_a_re, s5_a_im, s5_b_re, s5_b_im, s5_c_re, s5_c_im, s5_d, s5_log_dt, s5_w_glu, w_branch_a, w_branch_b, w_branch_c, w_out, router_w, router_b, moe_w_gate, moe_w_up, moe_w_down, final_norm_g):
    p = {
        'norm1_g': norm1_g, 'norm2_g': norm2_g, 'diff_lambda': diff_lambda, 'diff_subln_g': diff_subln_g,
        'hgrn_lb': hgrn_lb, 'hgrn_norm_g': hgrn_norm_g, 's5_a_re': s5_a_re, 's5_a_im': s5_a_im,
        's5_b_re': s5_b_re, 's5_b_im': s5_b_im, 's5_c_re': s5_c_re, 's5_c_im': s5_c_im, 's5_d': s5_d,
        's5_log_dt': s5_log_dt, 'router_w': router_w, 'router_b': router_b,
        'moe_w_gate': moe_w_gate, 'moe_w_up': moe_w_up, 'moe_w_down': moe_w_down, 'final_norm_g': final_norm_g,
    }
    wts = {'w_in': w_in.astype(BF16), 'w_glu': s5_w_glu.astype(BF16), 'w_a': w_branch_a.astype(BF16),
           'w_b': w_branch_b.astype(BF16), 'w_c': w_branch_c.astype(BF16), 'w_out': w_out.astype(BF16)}
    bp, tp, d = x_prompt.shape
    bs, ts, _ = x_sample.shape
    depth = w_mod.shape[0]
    mods = _modulation(jnp.concatenate([c_prompt, c_sample], axis=0), w_mod, b_mod)
    mods = mods.reshape(depth, bp + bs, 1, 6 * d)
    pos_p = jnp.arange(tp)
    pos_s = cache_k.shape[2] + jnp.arange(ts)
    tiles_p = dict(tm=min(256, tp), tq=min(256, tp), tb=min(512, tp), tc=min(64, tp),
                   moe_nb=1, moe_tb=min(1024, tp))
    tiles_s = dict(tm=ts, tq=ts, tb=ts, tc=ts, moe_nb=bs, moe_tb=ts)
    y_p, k_p, v_p, h_p, re_p, im_p = _trunk(x_prompt, mods[:, :bp], pos_p, p, wts, None, tiles_p)
    y_s, k_s, v_s, h_s, re_s, im_s = _trunk(x_sample, mods[:, bp:], pos_s, p, wts,
                                            (cache_k, cache_v, state_hgrn, state_s5_re, state_s5_im), tiles_s)
    return (y_p, y_s, k_p, v_p, h_p, re_p, im_p, k_s, v_s, h_s, re_s, im_s)
```

```python
import functools
import math

import numpy as np
import jax
import jax.numpy as jnp
from jax import lax
from jax.experimental import pallas as pl
from jax.experimental.pallas import tpu as pltpu

F32 = jnp.float32
BF16 = jnp.bfloat16

D_MODEL = 1024
CHUNK = 64
EPS = 1e-6
MASK_NEG = -1e30
ROPE_THETA = 10000.0
A_HEADS = 4
A_DH = 64
A_DV = 128
A_QK = 512
B_HEADS = 4
B_DK = 128
C_WIDTH = 512
C_GROUP = 16
C_NGROUPS = 32
C_STATE = 64
C_LANES = C_NGROUPS * C_STATE
N_EXPERTS = 16
N_EXPERT_GROUPS = 4
EXPERTS_PER_GROUP = 4
EXPERT_DFF = 512
IN_COLS = 7168
COL_AK, COL_AV, COL_B, COL_CU, COL_G = 512, 1024, 1536, 3584, 4096

LANE = 128
VMEM_LIMIT = 56 * 1024 * 1024


def _cparams(sem):
    return pltpu.CompilerParams(dimension_semantics=sem, vmem_limit_bytes=VMEM_LIMIT)


def _sigmoid(x):
    return 0.5 * jnp.tanh(0.5 * x) + 0.5


def _sigmoid_rel(x):
    return jax.nn.sigmoid(x)


def _dot(a, b):
    return jnp.dot(a, b, preferred_element_type=F32)


def _dot_nt(a, b):
    return lax.dot_general(a, b, (((1,), (1,)), ((), ())), preferred_element_type=F32)


def _dot_tn(a, b):
    return lax.dot_general(a, b, (((0,), (0,)), ((), ())), preferred_element_type=F32)


def _mod_kernel(c_ref, w_ref, b_ref, o_ref):
    c = c_ref[...]
    a = (c * _sigmoid(c)).astype(BF16)
    o_ref[...] = _dot(a, w_ref[...].astype(BF16)) + b_ref[...]


def _modulation(c_all, w_mod, b_mod):
    depth, d, n6 = w_mod.shape
    nb = c_all.shape[0]
    tn = 1536
    return pl.pallas_call(
        _mod_kernel,
        grid=(depth, n6 // tn),
        in_specs=[
            pl.BlockSpec((nb, d), lambda l, j: (0, 0)),
            pl.BlockSpec((None, d, tn), lambda l, j: (l, 0, j)),
            pl.BlockSpec((None, 1, tn), lambda l, j: (l, 0, j)),
        ],
        out_specs=pl.BlockSpec((None, nb, tn), lambda l, j: (l, 0, j)),
        out_shape=jax.ShapeDtypeStruct((depth, nb, n6), F32),
        compiler_params=_cparams(("parallel", "parallel")),
        name="modulation",
    )(c_all, w_mod, b_mod.reshape(depth, 1, n6))


def _swap_half(a):
    parts = []
    for j in range(a.shape[1] // LANE):
        s = a[:, LANE * j:LANE * (j + 1)]
        lane = lax.broadcasted_iota(jnp.int32, s.shape, 1)
        first = (lane & (A_DH - 1)) < (A_DH // 2)
        parts.append(jnp.where(first, pltpu.roll(s, LANE - A_DH // 2, 1), pltpu.roll(s, A_DH // 2, 1)))
    return jnp.concatenate(parts, axis=1)


def _in_kernel(x_ref, mod_ref, g_ref, w_ref, cos_ref, sin_ref,
               q_ref, k_ref, v_ref, b_ref, cu_ref, gt_ref):
    x = x_ref[...]
    ms = jnp.mean(x * x, axis=-1, keepdims=True)
    y = x * lax.rsqrt(ms + EPS) * g_ref[...]
    h = y * (1.0 + mod_ref[:, D_MODEL:2 * D_MODEL]) + mod_ref[:, 0:D_MODEL]
    hb = h.astype(BF16)

    def proj(c0, width):
        return _dot(hb, w_ref[:, c0:c0 + width])

    cos = cos_ref[...]
    sin = sin_ref[...]

    def rope(a):
        return a * cos + _swap_half(a) * sin

    q_ref[...] = (rope(proj(0, A_QK)) * (A_DH ** -0.5)).astype(BF16)
    k_ref[...] = rope(proj(COL_AK, A_QK))
    v_ref[...] = proj(COL_AV, 512)
    for j in range(4):
        b_ref[:, 512 * j:512 * (j + 1)] = proj(COL_B + 512 * j, 512)
    cu_ref[...] = proj(COL_CU, C_WIDTH)
    for j in range(6):
        gt_ref[:, 512 * j:512 * (j + 1)] = proj(COL_G + 512 * j, 512)


def _in_projection(x, mod, g1, w_in, cos, sin, tm):
    bsz, t, d = x.shape
    nt = t // tm
    row = lambda b, i: (b, i, 0)
    return pl.pallas_call(
        _in_kernel,
        grid=(bsz, nt),
        in_specs=[
            pl.BlockSpec((None, tm, d), row),
            pl.BlockSpec((None, 1, 6 * d), lambda b, i: (b, 0, 0)),
            pl.BlockSpec((1, d), lambda b, i: (0, 0)),
            pl.BlockSpec((d, IN_COLS), lambda b, i: (0, 0), pipeline_mode=pl.Buffered(1)),
            pl.BlockSpec((tm, A_QK), lambda b, i: (i, 0)),
            pl.BlockSpec((tm, A_QK), lambda b, i: (i, 0)),
        ],
        out_specs=[
            pl.BlockSpec((None, tm, A_QK), row),
            pl.BlockSpec((None, tm, A_QK), row),
            pl.BlockSpec((None, tm, 512), row),
            pl.BlockSpec((None, tm, 2048), row),
            pl.BlockSpec((tm, C_WIDTH), lambda b, i: (i, b)),
            pl.BlockSpec((None, tm, 3 * d), row),
        ],
        out_shape=[
            jax.ShapeDtypeStruct((bsz, t, A_QK), BF16),
            jax.ShapeDtypeStruct((bsz, t, A_QK), F32),
            jax.ShapeDtypeStruct((bsz, t, 512), F32),
            jax.ShapeDtypeStruct((bsz, t, 2048), F32),
            jax.ShapeDtypeStruct((t, bsz * C_WIDTH), F32),
            jax.ShapeDtypeStruct((bsz, t, 3 * d), F32),
        ],
        compiler_params=_cparams(("parallel", "parallel")),
        name="in_projection",
    )(x, mod, g1, w_in, cos, sin)


def _split_maps(q):
    lane = lax.broadcasted_iota(jnp.int32, q.shape, 1)
    zero = jnp.zeros_like(q)
    return jnp.concatenate([jnp.where(lane < A_DH, q, zero), jnp.where(lane >= A_DH, q, zero)], axis=0)


def _attn_finish(acc, l, lam, g, scale_out, tq):
    o = acc[:tq] / l[:tq] - lam * (acc[tq:] / l[tq:])
    ms = jnp.mean(o * o, axis=-1, keepdims=True)
    return (o * lax.rsqrt(ms + EPS) * g) * scale_out


def _attn_prompt_kernel(lam_ref, q_ref, k_ref, v_ref, g_ref, o_ref, kb, vt, *, tq, scale_out):
    i = pl.program_id(1)
    nkv = kb.shape[0] // A_HEADS

    @pl.when(i == 0)
    def _():
        for h in range(A_HEADS):
            for jj in range(nkv):
                blk = (slice(jj * tq, (jj + 1) * tq), slice(h * LANE, (h + 1) * LANE))
                kb[h * nkv + jj] = k_ref[blk].astype(BF16)
                vt[h * nkv + jj] = v_ref[blk].T.astype(BF16)

    qq = [_split_maps(q_ref[:, h * LANE:(h + 1) * LANE]) for h in range(A_HEADS)]

    def tile(j, carry, h, masked):
        m, l, acc = carry
        s = _dot_nt(kb[h * nkv + j], qq[h])
        if masked:
            kr = lax.broadcasted_iota(jnp.int32, s.shape, 0)
            qc = lax.broadcasted_iota(jnp.int32, s.shape, 1)
            s = jnp.where((kr >> 6) <= ((qc & (tq - 1)) >> 6), s, MASK_NEG)
        mn = jnp.maximum(m, jnp.max(s, axis=0, keepdims=True))
        a = jnp.exp(m - mn)
        p = jnp.exp(s - mn)
        l = a * l + jnp.sum(p, axis=0, keepdims=True)
        acc = a * acc + _dot(vt[h * nkv + j], p.astype(BF16))
        return mn, l, acc

    def tiles(j, carries, masked):
        return tuple(tile(j, carries[h], h, masked) for h in range(A_HEADS))

    init = (jnp.full((1, 2 * tq), MASK_NEG, F32), jnp.zeros((1, 2 * tq), F32), jnp.zeros((A_DV, 2 * tq), F32))
    carries = lax.fori_loop(0, i, lambda j, c: tiles(j, c, False), (init,) * A_HEADS)
    carries = tiles(i, carries, True)
    for h in range(A_HEADS):
        _, l, acc = carries[h]
        acc = acc / l
        o = acc[:, :tq] - lam_ref[0:1, 0:1] * acc[:, tq:]
        ms = jnp.mean(o * o, axis=0, keepdims=True)
        y = (o * lax.rsqrt(ms + EPS) * g_ref[...]) * scale_out
        o_ref[:, h * LANE:(h + 1) * LANE] = y.T.astype(BF16)


def _attn_prompt(lam, q, k, v, g, scale_out, tq):
    bsz, t, _ = q.shape
    assert tq % CHUNK == 0 and tq & (tq - 1) == 0
    kern = functools.partial(_attn_prompt_kernel, tq=tq, scale_out=scale_out)
    return pl.pallas_call(
        kern,
        grid=(bsz, t // tq),
        in_specs=[
            pl.BlockSpec((1, LANE), lambda b, i: (0, 0)),
            pl.BlockSpec((None, tq, A_HEADS * LANE), lambda b, i: (b, i, 0)),
            pl.BlockSpec((None, t, A_HEADS * LANE), lambda b, i: (b, 0, 0)),
            pl.BlockSpec((None, t, A_HEADS * LANE), lambda b, i: (b, 0, 0)),
            pl.BlockSpec((A_DV, 1), lambda b, i: (0, 0)),
        ],
        out_specs=pl.BlockSpec((None, tq, A_HEADS * LANE), lambda b, i: (b, i, 0)),
        out_shape=jax.ShapeDtypeStruct((bsz, t, A_HEADS * A_DV), BF16),
        scratch_shapes=[pltpu.VMEM((A_HEADS * (t // tq), tq, LANE), BF16),
                        pltpu.VMEM((A_HEADS * (t // tq), A_DV, tq), BF16)],
        compiler_params=_cparams(("parallel", "arbitrary")),
        name="attn_prompt",
    )(lam, q, k, v, g.reshape(A_DV, 1))


def _attn_sample_kernel(lam_ref, q_ref, pk_ref, pv_ref, k_ref, v_ref, g_ref, o_ref, *, tq, scale_out):
    qq = _split_maps(q_ref[...])
    s_p = _dot_nt(qq, pk_ref[...].astype(BF16))
    s_n = _dot_nt(qq, k_ref[...].astype(BF16))
    m = jnp.maximum(jnp.max(s_p, axis=-1, keepdims=True), jnp.max(s_n, axis=-1, keepdims=True))
    p_p = jnp.exp(s_p - m)
    p_n = jnp.exp(s_n - m)
    l = jnp.sum(p_p, axis=-1, keepdims=True) + jnp.sum(p_n, axis=-1, keepdims=True)
    acc = _dot(p_p.astype(BF16), pv_ref[...].astype(BF16)) + _dot(p_n.astype(BF16), v_ref[...].astype(BF16))
    o_ref[...] = _attn_finish(acc, l, lam_ref[...], g_ref[...], scale_out, tq).astype(BF16)


def _attn_sample(lam, q, cache_k, cache_v, layer, k, v, g, scale_out):
    bsz, t, _ = q.shape
    past = cache_k.shape[2]
    kern = functools.partial(_attn_sample_kernel, tq=t, scale_out=scale_out)
    new = lambda b, h: (b, 0, h)
    old = lambda b, h: (layer, b, 0, h)
    return pl.pallas_call(
        kern,
        grid=(bsz, A_HEADS),
        in_specs=[
            pl.BlockSpec((1, LANE), lambda b, h: (0, 0)),
            pl.BlockSpec((None, t, LANE), new),
            pl.BlockSpec((None, None, past, LANE), old),
            pl.BlockSpec((None, None, past, LANE), old),
            pl.BlockSpec((None, t, LANE), new),
            pl.BlockSpec((None, t, LANE), new),
            pl.BlockSpec((1, LANE), lambda b, h: (0, 0)),
        ],
        out_specs=pl.BlockSpec((None, t, LANE), new),
        out_shape=jax.ShapeDtypeStruct((bsz, t, A_HEADS * A_DV), BF16),
        compiler_params=_cparams(("parallel", "parallel")),
        name="attn_sample",
    )(lam, q, cache_k, cache_v, k, v, g)


def _split3(x):
    hi = x.astype(BF16)
    r = x - hi.astype(F32)
    mid = r.astype(BF16)
    lo = (r - mid.astype(F32)).astype(BF16)
    return hi, mid, lo


HGRN_LEVELS = (5, 4, 3)
HGRN_DIAG = 8


def _hgrn_tables():
    L = CHUNK
    r = np.arange(L)[:, None]
    c = np.arange(L)[None, :]
    sgn, up, pair = [], [], []
    for sh in HGRN_LEVELS:
        upper = ((r >> sh) & 1) == 1
        up.append(np.broadcast_to(upper, (L, B_DK)))
        sgn.append(np.broadcast_to(np.where(upper, 1.0, -1.0), (L, B_DK)))
        pair.append((((r >> sh) & 1) == 1) & ((r >> (sh + 1)) == (c >> (sh + 1))) & (((c >> sh) & 1) == 0))
    diag = [(c == r - d) & ((r & (HGRN_DIAG - 1)) >= d) for d in range(HGRN_DIAG)]
    tril = (c <= r).astype(np.float32)
    rowm = jnp.asarray(np.stack(sgn + up).astype(np.float32))
    sqm = jnp.asarray(np.stack(pair + diag).astype(np.float32))
    tril3 = jnp.asarray(np.concatenate([tril, tril, tril], axis=1), dtype=BF16)
    return tril3, rowm, sqm


def _hgrn_chunk(qraw, fraw, v, graw, lb, gn, st, tril3, rowm_ref, sqm_ref):
    L = CHUNK
    nl = len(HGRN_LEVELS)
    f = lb + (1.0 - lb) * _sigmoid_rel(fraw)
    logf = jnp.log(f)
    kf = 1.0 - f
    q = qraw * _sigmoid(qraw)

    hi, mid, lo = _split3(logf)
    b = _dot(tril3, jnp.concatenate([hi, mid, lo], axis=0))

    vb = v.astype(BF16)
    o = _dot_nt((q * jnp.exp(b)).astype(BF16), st.astype(BF16))

    att = jnp.zeros((L, L), F32)
    for li, sh in enumerate(HGRN_LEVELS):
        m = 1 << sh
        ref_rows = [jnp.broadcast_to(b[(2 * j + 1) * m - 1:(2 * j + 1) * m, :], (2 * m, B_DK))
                    for j in range(L // (2 * m))]
        ref = ref_rows[0] if len(ref_rows) == 1 else jnp.concatenate(ref_rows, axis=0)
        w = jnp.exp((b - ref) * rowm_ref[li])
        wq = w * rowm_ref[nl + li]
        att = att + sqm_ref[li] * _dot_nt((q * wq).astype(BF16), (kf * (w - wq)).astype(BF16))
    hd = kf
    for d in range(HGRN_DIAG):
        if d > 0:
            hd = f * pltpu.roll(hd, 1, 0)
        att = att + sqm_ref[nl + d] * jnp.sum(q * hd, axis=-1, keepdims=True)
    o = o + _dot(att.astype(BF16), vb)

    b_last = b[L - 1:L, :]
    kdec = (kf * jnp.exp(b_last - b)).astype(BF16)
    st_new = st * jnp.exp(b_last) + _dot_tn(vb, kdec)

    ms = jnp.mean(o * o, axis=-1, keepdims=True)
    y = o * lax.rsqrt(ms + EPS) * gn * (graw * _sigmoid(graw))
    return y, st_new


def _hgrn_kernel(*refs, nchunk, has_s0):
    if has_s0:
        q_ref, f_ref, v_ref, g_ref, lb_ref, gn_ref, tril_ref, rowm_ref, sqm_ref, s0_ref, y_ref, sf_ref, st = refs
    else:
        q_ref, f_ref, v_ref, g_ref, lb_ref, gn_ref, tril_ref, rowm_ref, sqm_ref, y_ref, sf_ref, st = refs
    c = pl.program_id(1)

    @pl.when(c == 0)
    def _():
        for h in range(B_HEADS):
            st[h] = s0_ref[h].T if has_s0 else jnp.zeros((LANE, B_DK), F32)

    gn = gn_ref[...]
    tril3 = tril_ref[...]

    def body(n, carry):
        sl = pl.ds(pl.multiple_of(n * CHUNK, CHUNK), CHUNK)
        for h in range(B_HEADS):
            hs = slice(h * LANE, (h + 1) * LANE)
            y, st_new = _hgrn_chunk(q_ref[sl, hs], f_ref[sl, hs], v_ref[sl, hs], g_ref[sl, hs],
                                    lb_ref[:, hs], gn, st[h], tril3, rowm_ref, sqm_ref)
            st[h] = st_new
            y_ref[sl, hs] = y.astype(BF16)
        return carry

    lax.fori_loop(0, nchunk, body, 0, unroll=min(2, nchunk))

    @pl.when(c == pl.num_programs(1) - 1)
    def _():
        for h in range(B_HEADS):
            sf_ref[h] = st[h].T


def _hgrn(braw, lb, gn, s0, layer, tb):
    bsz, t, _ = braw.shape
    has_s0 = s0 is not None
    kern = functools.partial(_hgrn_kernel, nchunk=tb // CHUNK, has_s0=has_s0)
    width = B_HEADS * LANE
    tril3, rowm, sqm = _hgrn_tables()

    def col(j):
        return pl.BlockSpec((None, tb, width), lambda b, c: (b, c, j))

    def const(shape):
        return pl.BlockSpec(shape, lambda b, c: (0,) * len(shape))

    in_specs = [col(0), col(1), col(2), col(3), const((1, width)), const((1, LANE)),
                const(tril3.shape), const(rowm.shape), const(sqm.shape)]
    args = [braw, braw, braw, braw, lb, gn, tril3, rowm, sqm]
    if has_s0:
        in_specs.append(pl.BlockSpec((None, None, B_HEADS, B_DK, LANE), lambda b, c: (layer, b, 0, 0, 0)))
        args.append(s0)
    return pl.pallas_call(
        kern,
        grid=(bsz, t // tb),
        in_specs=in_specs,
        out_specs=[
            pl.BlockSpec((None, tb, width), lambda b, c: (b, c, 0)),
            pl.BlockSpec((None, B_HEADS, B_DK, LANE), lambda b, c: (b, 0, 0, 0)),
        ],
        out_shape=[
            jax.ShapeDtypeStruct((bsz, t, width), BF16),
            jax.ShapeDtypeStruct((bsz, B_HEADS, B_DK, LANE), F32),
        ],
        scratch_shapes=[pltpu.VMEM((B_HEADS, LANE, B_DK), F32)],
        compiler_params=_cparams(("parallel", "arbitrary")),
        name="hgrn",
    )(*args)


S5_COLS = 512


def _s5_kernel(u_ref, bm_ref, cm_ref, a_ref, d_ref, h0_ref, y_ref, hf_ref, xs, hs, *, tc, nb):
    step_id = pl.program_id(0)

    @pl.when(step_id == 0)
    def _():
        hs[...] = h0_ref[...]

    u = u_ref[...]
    xs[...] = _dot(u.astype(BF16), bm_ref[...])
    for cc in range(C_LANES // S5_COLS):
        re = slice(cc * S5_COLS, (cc + 1) * S5_COLS)
        im = slice(C_LANES + cc * S5_COLS, C_LANES + (cc + 1) * S5_COLS)
        ar = a_ref[0:nb, re]
        ai = a_ref[nb:2 * nb, re]

        def step(t, carry):
            hr, hi = carry
            rows = pl.ds(pl.multiple_of(t * nb, nb), nb)
            nr = ar * hr - ai * hi + xs[rows, re]
            ni = ar * hi + ai * hr + xs[rows, im]
            xs[rows, re] = nr
            xs[rows, im] = ni
            return nr, ni

        hr, hi = lax.fori_loop(0, tc, step, (hs[:, re], hs[:, im]), unroll=8)
        hs[:, re] = hr
        hs[:, im] = hi
    y = _dot(xs[...].astype(BF16), cm_ref[...]) + d_ref[...] * u
    y_ref[...] = y.astype(BF16)
    hf_ref[...] = hs[...]


def _s5(u_tm, bm, cm, a, d, h0, tc):
    n, _ = u_tm.shape
    nb = h0.shape[0]
    rows = tc * nb
    kern = functools.partial(_s5_kernel, tc=tc, nb=nb)
    const = lambda i: (0, 0)
    return pl.pallas_call(
        kern,
        grid=(n // rows,),
        in_specs=[
            pl.BlockSpec((rows, C_WIDTH), lambda i: (i, 0)),
            pl.BlockSpec((C_WIDTH, 2 * C_LANES), const, pipeline_mode=pl.Buffered(1)),
            pl.BlockSpec((2 * C_LANES, C_WIDTH), const, pipeline_mode=pl.Buffered(1)),
            pl.BlockSpec((2 * nb, C_LANES), const),
            pl.BlockSpec((1, C_WIDTH), const),
            pl.BlockSpec((nb, 2 * C_LANES), const),
        ],
        out_specs=[
            pl.BlockSpec((rows, C_WIDTH), lambda i: (i, 0)),
            pl.BlockSpec((nb, 2 * C_LANES), const),
        ],
        out_shape=[
            jax.ShapeDtypeStruct((n, C_WIDTH), BF16),
            jax.ShapeDtypeStruct((nb, 2 * C_LANES), F32),
        ],
        scratch_shapes=[pltpu.VMEM((rows, 2 * C_LANES), F32), pltpu.VMEM((nb, 2 * C_LANES), F32)],
        compiler_params=_cparams(("arbitrary",)),
        name="s5",
    )(u_tm, bm, cm, a, d, h0)


def _s5_tables(a_re, a_im, b_re, b_im, c_re, c_im, log_dt, nb):
    dt = jnp.exp(log_dt)[:, None]
    mag = jnp.exp(a_re * dt)
    abr, abi = mag * jnp.cos(a_im * dt), mag * jnp.sin(a_im * dt)
    den = a_re * a_re + a_im * a_im
    coef_r = ((abr - 1.0) * a_re + abi * a_im) / den
    coef_i = (abi * a_re - (abr - 1.0) * a_im) / den
    fr = coef_r[..., None] * b_re - coef_i[..., None] * b_im
    fi = coef_r[..., None] * b_im + coef_i[..., None] * b_re
    eye = jnp.eye(C_NGROUPS, dtype=F32)

    def blockdiag_in(m):
        return jnp.einsum('gpj,gh->gjhp', m, eye).reshape(C_WIDTH, C_LANES)

    def blockdiag_out(m):
        return jnp.einsum('gjp,gh->gphj', m, eye).reshape(C_LANES, C_WIDTH)

    bm = jnp.concatenate([blockdiag_in(fr), blockdiag_in(fi)], axis=1).astype(BF16)
    cm = jnp.concatenate([blockdiag_out(c_re), -blockdiag_out(c_im)], axis=0).astype(BF16)
    a = jnp.concatenate([jnp.broadcast_to(abr.reshape(1, C_LANES), (nb, C_LANES)),
                         jnp.broadcast_to(abi.reshape(1, C_LANES), (nb, C_LANES))], axis=0)
    return bm, cm, a


def _pair_max_sum(v):
    best = v[0:1] + v[1:2]
    for i, j in ((0, 2), (0, 3), (1, 2), (1, 3), (2, 3)):
        best = jnp.maximum(best, v[i:i + 1] + v[j:j + 1])
    return best


def _route(logits_t, rb):
    mx = jnp.max(logits_t, axis=0, keepdims=True)
    ex = jnp.exp(logits_t - mx)
    s = ex / jnp.sum(ex, axis=0, keepdims=True)
    sb = s + rb
    g = EXPERTS_PER_GROUP
    best = _pair_max_sum(sb[0:g])
    sel = jnp.zeros(best.shape, jnp.int32)
    for gi in range(1, N_EXPERT_GROUPS):
        sc = _pair_max_sum(sb[gi * g:(gi + 1) * g])
        take = sc > best
        best = jnp.where(take, sc, best)
        sel = jnp.where(take, gi, sel)
    sb4 = jnp.zeros((g,) + best.shape[1:], F32)
    s4 = jnp.zeros((g,) + best.shape[1:], F32)
    for gi in range(N_EXPERT_GROUPS):
        on = sel == gi
        sb4 = jnp.where(on, sb[gi * g:(gi + 1) * g], sb4)
        s4 = jnp.where(on, s[gi * g:(gi + 1) * g], s4)
    rows = []
    for e in range(g):
        rank = jnp.zeros(best.shape, jnp.int32)
        for j in range(g):
            if j == e:
                continue
            ahead = (sb4[j:j + 1] > sb4[e:e + 1]) if j > e else (sb4[j:j + 1] >= sb4[e:e + 1])
            rank = rank + ahead.astype(jnp.int32)
        rows.append(jnp.where(rank < 2, s4[e:e + 1], 0.0))
    w4 = jnp.concatenate(rows, axis=0)
    w4 = w4 / jnp.sum(w4, axis=0, keepdims=True)
    return jnp.concatenate([jnp.where(sel == gi, w4, 0.0) for gi in range(N_EXPERT_GROUPS)], axis=0)


def _merge_kernel(x_ref, ya_ref, yb_ref, ys_ref, gt_ref, mod_ref, g2_ref,
                  wglu_ref, wa_ref, wb_ref, wc_ref, wout_ref, rw_ref, rb_ref,
                  x1_ref, h2_ref, gates_ref):
    d = D_MODEL
    z = _dot(ys_ref[...], wglu_ref[...])
    yc = (z[:, :C_WIDTH] * _sigmoid(z[:, C_WIDTH:])).astype(BF16)
    merged = (_sigmoid(gt_ref[:, 0:d]) * _dot(ya_ref[...], wa_ref[...])
              + _sigmoid(gt_ref[:, d:2 * d]) * _dot(yb_ref[...], wb_ref[...])
              + _sigmoid(gt_ref[:, 2 * d:3 * d]) * _dot(yc, wc_ref[...]))
    mix = _dot(merged.astype(BF16), wout_ref[...])
    x1 = x_ref[...] + mod_ref[:, 2 * d:3 * d] * mix
    x1_ref[...] = x1
    ms = jnp.mean(x1 * x1, axis=-1, keepdims=True)
    h2 = (x1 * lax.rsqrt(ms + EPS) * g2_ref[...]) * (1.0 + mod_ref[:, 4 * d:5 * d]) + mod_ref[:, 3 * d:4 * d]
    h_hi, h_mid, h_lo = _split3(h2)
    h2_ref[...] = h_hi
    hcat = jnp.concatenate([h_hi, h_mid, h_hi, h_lo, h_mid, h_hi], axis=1)
    gates_t = _route(_dot_nt(rw_ref[...], hcat), rb_ref[...])
    pad = jnp.zeros((LANE - N_EXPERTS, gates_t.shape[1]), F32)
    gates_ref[...] = jnp.concatenate([gates_t, pad], axis=0).T


def _merge(x, ya, yb, ys_tm, graw, mod, g2, wglu, wa, wb, wc, wout, rw_t, rb, tm):
    bsz, t, d = x.shape
    row = lambda b, i: (b, i, 0)
    const = lambda b, i: (0, 0)

    def wspec(shape):
        return pl.BlockSpec(shape, const, pipeline_mode=pl.Buffered(1))

    return pl.pallas_call(
        _merge_kernel,
        grid=(bsz, t // tm),
        in_specs=[
            pl.BlockSpec((None, tm, d), row),
            pl.BlockSpec((None, tm, 512), row),
            pl.BlockSpec((None, tm, 512), row),
            pl.BlockSpec((tm, C_WIDTH), lambda b, i: (i, b)),
            pl.BlockSpec((None, tm, 3 * d), row),
            pl.BlockSpec((None, 1, 6 * d), lambda b, i: (b, 0, 0)),
            pl.BlockSpec((1, d), const),
            wspec((C_WIDTH, 2 * C_WIDTH)), wspec((512, d)), wspec((512, d)), wspec((C_WIDTH, d)),
            wspec((d, d)), wspec((N_EXPERTS, 6 * d)), wspec((N_EXPERTS, 1)),
        ],
        out_specs=[
            pl.BlockSpec((None, tm, d), row),
            pl.BlockSpec((None, tm, d), row),
            pl.BlockSpec((None, tm, LANE), row),
        ],
        out_shape=[
            jax.ShapeDtypeStruct((bsz, t, d), F32),
            jax.ShapeDtypeStruct((bsz, t, d), BF16),
            jax.ShapeDtypeStruct((bsz, t, LANE), F32),
        ],
        compiler_params=_cparams(("parallel", "parallel")),
        name="merge_router",
    )(x, ya, yb, ys_tm, graw, mod, g2, wglu, wa, wb, wc, wout, rw_t, rb)


def _moe_kernel(h_ref, gates_ref, x1_ref, mod_ref, wg_ref, wu_ref, wd_ref, gf_ref, o_ref, acc, *, final):
    e = pl.program_id(1)
    nbt, tb, d = h_ref.shape
    tm = nbt * tb

    @pl.when(e == 0)
    def _():
        acc[...] = jnp.zeros_like(acc)

    h = h_ref[...].reshape(tm, d)
    a = _dot(h, wg_ref[...].astype(BF16))
    u = _dot(h, wu_ref[...].astype(BF16))
    gates = gates_ref[...].reshape(tm, LANE)
    lane = lax.broadcasted_iota(jnp.int32, gates.shape, 1)
    gcol = jnp.sum(jnp.where(lane == e, gates, 0.0), axis=-1, keepdims=True)
    hid = (a * _sigmoid(a)) * u * gcol
    acc[...] += _dot(hid.astype(BF16), wd_ref[...].astype(BF16))

    @pl.when(e == pl.num_programs(1) - 1)
    def _():
        x2 = x1_ref[...] + mod_ref[:, :, 5 * d:6 * d] * acc[...].reshape(nbt, tb, d)
        if final:
            ms = jnp.mean(x2 * x2, axis=-1, keepdims=True)
            x2 = x2 * lax.rsqrt(ms + EPS) * gf_ref[...]
        o_ref[...] = x2


def _moe(h2, gates, x1, mod, wg, wu, wd, gfinal, layer, nbt, tb, final):
    bsz, t, d = x1.shape
    kern = functools.partial(_moe_kernel, final=final)
    tok = lambda i, e: (i // (t // tb), i % (t // tb), 0) if nbt == 1 else (i, 0, 0)
    nsteps = (bsz // nbt) * (t // tb)
    return pl.pallas_call(
        kern,
        grid=(nsteps, N_EXPERTS),
        in_specs=[
            pl.BlockSpec((nbt, tb, d), tok),
            pl.BlockSpec((nbt, tb, LANE), tok),
            pl.BlockSpec((nbt, tb, d), tok),
            pl.BlockSpec((nbt, 1, 6 * d), lambda i, e: ((i // (t // tb)) if nbt == 1 else i, 0, 0)),
            pl.BlockSpec((None, None, d, EXPERT_DFF), lambda i, e: (layer, e, 0, 0)),
            pl.BlockSpec((None, None, d, EXPERT_DFF), lambda i, e: (layer, e, 0, 0)),
            pl.BlockSpec((None, None, EXPERT_DFF, d), lambda i, e: (layer, e, 0, 0)),
            pl.BlockSpec((1, d), lambda i, e: (0, 0)),
        ],
        out_specs=pl.BlockSpec((nbt, tb, d), tok),
        out_shape=jax.ShapeDtypeStruct((bsz, t, d), F32),
        scratch_shapes=[pltpu.VMEM((nbt * tb, d), F32)],
        compiler_params=_cparams(("parallel", "arbitrary")),
        name="moe",
    )(h2, gates, x1, mod, wg, wu, wd, gfinal)


def _rope_tables(pos):
    inv = ROPE_THETA ** (-jnp.arange(0, A_DH, 2, dtype=F32) / A_DH)
    ang = pos.astype(F32)[:, None] * inv[None, :]
    cos, sin = jnp.cos(ang), jnp.sin(ang)
    seg_cos = jnp.concatenate([cos, cos], axis=1)
    seg_sin = jnp.concatenate([-sin, sin], axis=1)
    reps = A_QK // A_DH
    return jnp.tile(seg_cos, (1, reps)), jnp.tile(seg_sin, (1, reps))


def _trunk(x, mods, pos, p, wts, past, tiles):
    bsz, t, d = x.shape
    depth = mods.shape[0]
    cos, sin = _rope_tables(pos)
    lb_p = jax.nn.softmax(p['hgrn_lb'].astype(F32), axis=0)
    lbs = jnp.cumsum(lb_p, axis=0) - lb_p[0:1]
    w_hi, w_mid, w_lo = _split3(p['router_w'].T.astype(F32))
    rw_t = jnp.concatenate([w_hi, w_hi, w_mid, w_hi, w_mid, w_lo], axis=1)
    rb = p['router_b'].reshape(N_EXPERTS, 1).astype(F32)
    gfinal = p['final_norm_g'].reshape(1, d)
    ks, vs, hs, rs, ims = [], [], [], [], []
    for l in range(depth):
        mod = mods[l]
        q, k, v, braw, cu_tm, graw = _in_projection(
            x, mod, p['norm1_g'][l].reshape(1, d), wts['w_in'][l], cos, sin, tiles['tm'])

        lam_init = 0.8 - 0.6 * math.exp(-0.3 * l)
        lp = p['diff_lambda'][l].astype(F32)
        lam = jnp.exp(jnp.sum(lp[0] * lp[1])) - jnp.exp(jnp.sum(lp[2] * lp[3])) + lam_init
        lam_row = jnp.full((1, LANE), lam, F32)
        subg = p['diff_subln_g'][l].reshape(1, A_DV)
        if past is None:
            ya = _attn_prompt(lam_row, q, k, v, subg, 1.0 - lam_init, tiles['tq'])
        else:
            pk = past[0].reshape(depth, bsz, -1, A_HEADS * 2 * A_DH)
            pv = past[1].reshape(depth, bsz, -1, A_HEADS * A_DV)
            ya = _attn_sample(lam_row, q, pk, pv, l, k, v, subg, 1.0 - lam_init)

        yb, s_h = _hgrn(braw, lbs[l].reshape(1, B_HEADS * B_DK), p['hgrn_norm_g'][l].reshape(1, LANE),
                        None if past is None else past[2], l, tiles['tb'])

        bm, cm, a_tab = _s5_tables(p['s5_a_re'][l], p['s5_a_im'][l], p['s5_b_re'][l], p['s5_b_im'][l],
                                   p['s5_c_re'][l], p['s5_c_im'][l], p['s5_log_dt'][l], bsz)
        if past is None:
            h0 = jnp.zeros((bsz, 2 * C_LANES), F32)
        else:
            h0 = jnp.concatenate([past[3][l].reshape(bsz, C_LANES), past[4][l].reshape(bsz, C_LANES)], axis=1)
        ys_tm, s5_state = _s5(cu_tm.reshape(t * bsz, C_WIDTH), bm, cm, a_tab,
                              p['s5_d'][l].reshape(1, C_WIDTH), h0, tiles['tc'])

        x1, h2, gates = _merge(x, ya, yb, ys_tm.reshape(t, bsz * C_WIDTH), graw, mod,
                               p['norm2_g'][l].reshape(1, d), wts['w_glu'][l], wts['w_a'][l], wts['w_b'][l],
                               wts['w_c'][l], wts['w_out'][l], rw_t, rb, tiles['tm'])
        x = _moe(h2, gates, x1, mod, p['moe_w_gate'], p['moe_w_up'], p['moe_w_down'], gfinal, l,
                 tiles['moe_nb'], tiles['moe_tb'], final=(l == depth - 1))

        ks.append(k.reshape(bsz, t, A_HEADS, 2 * A_DH))
        vs.append(v.reshape(bsz, t, A_HEADS, A_DV))
        hs.append(s_h)
        rs.append(s5_state[:, :C_LANES].reshape(bsz, C_NGROUPS, C_STATE))
        ims.append(s5_state[:, C_LANES:].reshape(bsz, C_NGROUPS, C_STATE))
    return (x, jnp.stack(ks), jnp.stack(vs), jnp.stack(hs), jnp.stack(rs), jnp.stack(ims))


def kernel(x_prompt, x_sample, cache_k, cache_v, state_hgrn, state_s5_re, state_s5_im, c_prompt, c_sample, w_mod, b_mod, norm1_g, norm2_g, w_in, diff_lambda, diff_subln_g, hgrn_lb, hgrn_norm_g, s5_a_re, s5_a_im, s5_b_re, s5_b_im, s5_c_re, s5_c_im, s5_d, s5_log_dt, s5_w_glu, w_branch_a, w_branch_b, w_branch_c, w_out, router_w, router_b, moe_w_gate, moe_w_up, moe_w_down, final_norm_g):
    p = {
        'norm1_g': norm1_g, 'norm2_g': norm2_g, 'diff_lambda': diff_lambda, 'diff_subln_g': diff_subln_g,
        'hgrn_lb': hgrn_lb, 'hgrn_norm_g': hgrn_norm_g, 's5_a_re': s5_a_re, 's5_a_im': s5_a_im,
        's5_b_re': s5_b_re, 's5_b_im': s5_b_im, 's5_c_re': s5_c_re, 's5_c_im': s5_c_im, 's5_d': s5_d,
        's5_log_dt': s5_log_dt, 'router_w': router_w, 'router_b': router_b,
        'moe_w_gate': moe_w_gate, 'moe_w_up': moe_w_up, 'moe_w_down': moe_w_down, 'final_norm_g': final_norm_g,
    }
    wts = {'w_in': w_in.astype(BF16), 'w_glu': s5_w_glu.astype(BF16), 'w_a': w_branch_a.astype(BF16),
           'w_b': w_branch_b.astype(BF16), 'w_c': w_branch_c.astype(BF16), 'w_out': w_out.astype(BF16)}
    bp, tp, d = x_prompt.shape
    bs, ts, _ = x_sample.shape
    depth = w_mod.shape[0]
    mods = _modulation(jnp.concatenate([c_prompt, c_sample], axis=0), w_mod, b_mod)
    mods = mods.reshape(depth, bp + bs, 1, 6 * d)
    pos_p = jnp.arange(tp)
    pos_s = cache_k.shape[2] + jnp.arange(ts)
    tiles_p = dict(tm=min(256, tp), tq=min(256, tp), tb=min(512, tp), tc=min(64, tp),
                   moe_nb=1, moe_tb=min(1024, tp))
    tiles_s = dict(tm=ts, tq=ts, tb=ts, tc=ts, moe_nb=bs, moe_tb=ts)
    y_p, k_p, v_p, h_p, re_p, im_p = _trunk(x_prompt, mods[:, :bp], pos_p, p, wts, None, tiles_p)
    y_s, k_s, v_s, h_s, re_s, im_s = _trunk(x_sample, mods[:, bp:], pos_s, p, wts,
                                            (cache_k, cache_v, state_hgrn, state_s5_re, state_s5_im), tiles_s)
    return (y_p, y_s, k_p, v_p, h_p, re_p, im_p, k_s, v_s, h_s, re_s, im_s)
```

```python
import functools
import math

import numpy as np
import jax
import jax.numpy as jnp
from jax import lax
from jax.experimental import pallas as pl
from jax.experimental.pallas import tpu as pltpu

F32 = jnp.float32
BF16 = jnp.bfloat16

D_MODEL = 1024
CHUNK = 64
EPS = 1e-6
MASK_NEG = -1e30
ROPE_THETA = 10000.0
A_HEADS = 4
A_DH = 64
A_DV = 128
A_QK = 512
B_HEADS = 4
B_DK = 128
C_WIDTH = 512
C_GROUP = 16
C_NGROUPS = 32
C_STATE = 64
C_LANES = C_NGROUPS * C_STATE
N_EXPERTS = 16
N_EXPERT_GROUPS = 4
EXPERTS_PER_GROUP = 4
EXPERT_DFF = 512
IN_COLS = 7168
COL_AK, COL_AV, COL_B, COL_CU, COL_G = 512, 1024, 1536, 3584, 4096

LANE = 128
LOG2E = math.log2(math.e)
ONES_ROWS = 16
VMEM_LIMIT = 56 * 1024 * 1024


def _cparams(sem):
    return pltpu.CompilerParams(dimension_semantics=sem, vmem_limit_bytes=VMEM_LIMIT)


def _sigmoid(x):
    return 0.5 * jnp.tanh(0.5 * x) + 0.5


def _sigmoid_rel(x):
    return jax.nn.sigmoid(x)


def _dot(a, b):
    return jnp.dot(a, b, preferred_element_type=F32)


def _dot_nt(a, b):
    return lax.dot_general(a, b, (((1,), (1,)), ((), ())), preferred_element_type=F32)


def _dot_tn(a, b):
    return lax.dot_general(a, b, (((0,), (0,)), ((), ())), preferred_element_type=F32)


def _mod_kernel(c_ref, w_ref, b_ref, o_ref):
    c = c_ref[...]
    a = (c * _sigmoid(c)).astype(BF16)
    o_ref[...] = _dot(a, w_ref[...].astype(BF16)) + b_ref[...]


def _modulation(c_all, w_mod, b_mod):
    depth, d, n6 = w_mod.shape
    nb = c_all.shape[0]
    tn = 1536
    return pl.pallas_call(
        _mod_kernel,
        grid=(depth, n6 // tn),
        in_specs=[
            pl.BlockSpec((nb, d), lambda l, j: (0, 0)),
            pl.BlockSpec((None, d, tn), lambda l, j: (l, 0, j)),
            pl.BlockSpec((None, 1, tn), lambda l, j: (l, 0, j)),
        ],
        out_specs=pl.BlockSpec((None, nb, tn), lambda l, j: (l, 0, j)),
        out_shape=jax.ShapeDtypeStruct((depth, nb, n6), F32),
        compiler_params=_cparams(("parallel", "parallel")),
        name="modulation",
    )(c_all, w_mod, b_mod.reshape(depth, 1, n6))


def _swap_half(a):
    parts = []
    for j in range(a.shape[1] // LANE):
        s = a[:, LANE * j:LANE * (j + 1)]
        lane = lax.broadcasted_iota(jnp.int32, s.shape, 1)
        first = (lane & (A_DH - 1)) < (A_DH // 2)
        parts.append(jnp.where(first, pltpu.roll(s, LANE - A_DH // 2, 1), pltpu.roll(s, A_DH // 2, 1)))
    return jnp.concatenate(parts, axis=1)


def _in_kernel(*refs, aliased):
    if aliased:
        refs = refs[:6] + refs[8:]
    (x_ref, mod_ref, g_ref, w_ref, cos_ref, sin_ref,
     q_ref, kb_ref, vb_ref, k4_ref, v4_ref, b_ref, cu_ref, gt_ref) = refs
    x = x_ref[...]
    ms = jnp.mean(x * x, axis=-1, keepdims=True)
    y = x * lax.rsqrt(ms + EPS) * g_ref[...]
    h = y * (1.0 + mod_ref[:, D_MODEL:2 * D_MODEL]) + mod_ref[:, 0:D_MODEL]
    hb = h.astype(BF16)

    def proj(c0, width):
        return _dot(hb, w_ref[:, c0:c0 + width])

    cos = cos_ref[...]
    sin = sin_ref[...]

    def rope(a):
        return a * cos + _swap_half(a) * sin

    q_ref[...] = (rope(proj(0, A_QK)) * (A_DH ** -0.5 * LOG2E)).astype(BF16)
    k = rope(proj(COL_AK, A_QK))
    v = proj(COL_AV, 512)
    kb_ref[...] = k.astype(BF16)
    vb_ref[...] = v.astype(BF16)
    for h in range(A_HEADS):
        k4_ref[:, h, :] = k[:, h * LANE:(h + 1) * LANE]
        v4_ref[:, h, :] = v[:, h * LANE:(h + 1) * LANE]
    for j in range(4):
        b_ref[:, 512 * j:512 * (j + 1)] = proj(COL_B + 512 * j, 512)
    cu_ref[...] = proj(COL_CU, C_WIDTH)
    for j in range(6):
        gt_ref[:, 512 * j:512 * (j + 1)] = proj(COL_G + 512 * j, 512)


def _in_projection(x, mod, g1, w_in, cos, sin, tm, layer, depth, k_all, v_all):
    bsz, t, d = x.shape
    nt = t // tm
    row = lambda b, i: (b, i, 0)
    aliased = k_all is not None
    cache_spec = pl.BlockSpec((None, None, tm, A_HEADS, LANE), lambda b, i: (layer, b, i, 0, 0))
    cache_shape = jax.ShapeDtypeStruct((depth, bsz, t, A_HEADS, LANE), F32)
    in_specs = [
        pl.BlockSpec((None, tm, d), row),
        pl.BlockSpec((None, 1, 6 * d), lambda b, i: (b, 0, 0)),
        pl.BlockSpec((1, d), lambda b, i: (0, 0)),
        pl.BlockSpec((d, IN_COLS), lambda b, i: (0, 0), pipeline_mode=pl.Buffered(1)),
        pl.BlockSpec((tm, A_QK), lambda b, i: (i, 0)),
        pl.BlockSpec((tm, A_QK), lambda b, i: (i, 0)),
    ]
    args = [x, mod, g1, w_in, cos, sin]
    if aliased:
        in_specs += [pl.BlockSpec(memory_space=pl.ANY), pl.BlockSpec(memory_space=pl.ANY)]
        args += [k_all, v_all]
    return pl.pallas_call(
        functools.partial(_in_kernel, aliased=aliased),
        grid=(bsz, nt),
        in_specs=in_specs,
        out_specs=[
            pl.BlockSpec((None, tm, A_QK), row),
            pl.BlockSpec((None, tm, A_QK), row),
            pl.BlockSpec((None, tm, 512), row),
            cache_spec,
            cache_spec,
            pl.BlockSpec((None, tm, 2048), row),
            pl.BlockSpec((tm, C_WIDTH), lambda b, i: (i, b)),
            pl.BlockSpec((None, tm, 3 * d), row),
        ],
        out_shape=[
            jax.ShapeDtypeStruct((bsz, t, A_QK), BF16),
            jax.ShapeDtypeStruct((bsz, t, A_QK), BF16),
            jax.ShapeDtypeStruct((bsz, t, 512), BF16),
            cache_shape,
            cache_shape,
            jax.ShapeDtypeStruct((bsz, t, 2048), F32),
            jax.ShapeDtypeStruct((t, bsz * C_WIDTH), F32),
            jax.ShapeDtypeStruct((bsz, t, 3 * d), F32),
        ],
        input_output_aliases={6: 3, 7: 4} if aliased else {},
        compiler_params=_cparams(("parallel", "parallel")),
        name="in_projection",
    )(*args)


def _split_maps(q):
    lane = lax.broadcasted_iota(jnp.int32, q.shape, 1)
    zero = jnp.zeros_like(q)
    return jnp.concatenate([jnp.where(lane < A_DH, q, zero), jnp.where(lane >= A_DH, q, zero)], axis=0)


def _attn_finish(acc, l, lam, g, scale_out, tq):
    o = acc[:tq] / l[:tq] - lam * (acc[tq:] / l[tq:])
    ms = jnp.mean(o * o, axis=-1, keepdims=True)
    return (o * lax.rsqrt(ms + EPS) * g) * scale_out


def _attn_prompt_kernel(lam_ref, q_ref, k_ref, v_ref, g_ref, o_ref, vt, *, tq, scale_out):
    i = pl.program_id(1)
    nkv = vt.shape[0] // A_HEADS

    @pl.when(i == 0)
    def _():
        for h in range(A_HEADS):
            for jj in range(nkv):
                blk = (slice(jj * tq, (jj + 1) * tq), slice(h * LANE, (h + 1) * LANE))
                vt[h * nkv + jj] = jnp.concatenate(
                    [v_ref[blk].astype(F32).T.astype(BF16), jnp.ones((ONES_ROWS, tq), BF16)], axis=0)

    qq = [_split_maps(q_ref[:, h * LANE:(h + 1) * LANE]) for h in range(A_HEADS)]

    def scores(j, h):
        ks = k_ref[pl.ds(pl.multiple_of(j * tq, tq), tq), h * LANE:(h + 1) * LANE]
        return _dot_nt(ks, qq[h])

    def tile(j, carry, s, h, masked):
        m, acc = carry
        if masked:
            kr = lax.broadcasted_iota(jnp.int32, s.shape, 0)
            qc = lax.broadcasted_iota(jnp.int32, s.shape, 1)
            s = jnp.where((kr >> 6) <= ((qc & (tq - 1)) >> 6), s, MASK_NEG)
        mn = jnp.maximum(m, jnp.max(s, axis=0, keepdims=True))
        p = jnp.exp2((s - mn).astype(BF16))
        acc = jnp.exp2(m - mn) * acc + _dot(vt[h * nkv + j], p)
        return mn, acc

    def tiles(j, carries, masked):
        ss = [scores(j, h) for h in range(A_HEADS)]
        return tuple(tile(j, carries[h], ss[h], h, masked) for h in range(A_HEADS))

    init = (jnp.full((1, 2 * tq), MASK_NEG, F32), jnp.zeros((A_DV + ONES_ROWS, 2 * tq), F32))
    carries = lax.fori_loop(0, i, lambda j, c: tiles(j, c, False), (init,) * A_HEADS)
    carries = tiles(i, carries, True)
    for h in range(A_HEADS):
        acc = carries[h][1]
        acc = acc[:A_DV] / acc[A_DV:A_DV + 1]
        o = acc[:, :tq] - lam_ref[0:1, 0:1] * acc[:, tq:]
        ms = jnp.mean(o * o, axis=0, keepdims=True)
        y = (o * lax.rsqrt(ms + EPS) * g_ref[...]) * scale_out
        o_ref[:, h * LANE:(h + 1) * LANE] = y.T.astype(BF16)


def _attn_prompt(lam, q, k, v, g, scale_out, tq):
    bsz, t, _ = q.shape
    assert tq % CHUNK == 0 and tq & (tq - 1) == 0
    kern = functools.partial(_attn_prompt_kernel, tq=tq, scale_out=scale_out)
    return pl.pallas_call(
        kern,
        grid=(bsz, t // tq),
        in_specs=[
            pl.BlockSpec((1, LANE), lambda b, i: (0, 0)),
            pl.BlockSpec((None, tq, A_HEADS * LANE), lambda b, i: (b, i, 0)),
            pl.BlockSpec((None, t, A_HEADS * LANE), lambda b, i: (b, 0, 0)),
            pl.BlockSpec((None, t, A_HEADS * LANE), lambda b, i: (b, 0, 0)),
            pl.BlockSpec((A_DV, 1), lambda b, i: (0, 0)),
        ],
        out_specs=pl.BlockSpec((None, tq, A_HEADS * LANE), lambda b, i: (b, i, 0)),
        out_shape=jax.ShapeDtypeStruct((bsz, t, A_HEADS * A_DV), BF16),
        scratch_shapes=[pltpu.VMEM((A_HEADS * (t // tq), A_DV + ONES_ROWS, tq), BF16)],
        compiler_params=_cparams(("parallel", "arbitrary")),
        name="attn_prompt",
    )(lam, q, k, v, g.reshape(A_DV, 1))


def _attn_sample_kernel(lam_ref, q_ref, pk_ref, pv_ref, k_ref, v_ref, g_ref, o_ref, *, tq, scale_out):
    qq = _split_maps(q_ref[...])
    s_p = _dot_nt(qq, pk_ref[...].astype(BF16))
    s_n = _dot_nt(qq, k_ref[...])
    m = jnp.maximum(jnp.max(s_p, axis=-1, keepdims=True), jnp.max(s_n, axis=-1, keepdims=True))
    p_p = jnp.exp2(s_p - m)
    p_n = jnp.exp2(s_n - m)
    l = jnp.sum(p_p, axis=-1, keepdims=True) + jnp.sum(p_n, axis=-1, keepdims=True)
    acc = _dot(p_p.astype(BF16), pv_ref[...].astype(BF16)) + _dot(p_n.astype(BF16), v_ref[...])
    o_ref[...] = _attn_finish(acc, l, lam_ref[...], g_ref[...], scale_out, tq).astype(BF16)


def _attn_sample(lam, q, cache_k, cache_v, layer, k, v, g, scale_out):
    bsz, t, _ = q.shape
    past = cache_k.shape[2]
    kern = functools.partial(_attn_sample_kernel, tq=t, scale_out=scale_out)
    new = lambda b, h: (b, 0, h)
    old = lambda b, h: (layer, b, 0, h)
    return pl.pallas_call(
        kern,
        grid=(bsz, A_HEADS),
        in_specs=[
            pl.BlockSpec((1, LANE), lambda b, h: (0, 0)),
            pl.BlockSpec((None, t, LANE), new),
            pl.BlockSpec((None, None, past, LANE), old),
            pl.BlockSpec((None, None, past, LANE), old),
            pl.BlockSpec((None, t, LANE), new),
            pl.BlockSpec((None, t, LANE), new),
            pl.BlockSpec((1, LANE), lambda b, h: (0, 0)),
        ],
        out_specs=pl.BlockSpec((None, t, LANE), new),
        out_shape=jax.ShapeDtypeStruct((bsz, t, A_HEADS * A_DV), BF16),
        compiler_params=_cparams(("parallel", "parallel")),
        name="attn_sample",
    )(lam, q, cache_k, cache_v, k, v, g)


def _split3(x):
    hi = x.astype(BF16)
    r = x - hi.astype(F32)
    mid = r.astype(BF16)
    lo = (r - mid.astype(F32)).astype(BF16)
    return hi, mid, lo


HGRN_LEVELS = (5, 4, 3)
HGRN_DIAG = 8


def _hgrn_tables():
    L = CHUNK
    r = np.arange(L)[:, None]
    c = np.arange(L)[None, :]
    sgn, up, pair = [], [], []
    for sh in HGRN_LEVELS:
        upper = ((r >> sh) & 1) == 1
        up.append(np.broadcast_to(upper, (L, B_DK)))
        sgn.append(np.broadcast_to(np.where(upper, 1.0, -1.0), (L, B_DK)))
        pair.append((((r >> sh) & 1) == 1) & ((r >> (sh + 1)) == (c >> (sh + 1))) & (((c >> sh) & 1) == 0))
    diag = [(c == r - d) & ((r & (HGRN_DIAG - 1)) >= d) for d in range(HGRN_DIAG)]
    tril = (c <= r).astype(np.float32)
    rowm = jnp.asarray(np.stack(sgn + up).astype(np.float32))
    sqm = jnp.asarray(np.stack(pair + diag).astype(np.float32))
    tril3 = jnp.asarray(np.concatenate([tril, tril, tril], axis=1), dtype=BF16)
    return tril3, rowm, sqm


def _hgrn_chunk(heads, gn, tril3, rowm_ref, sqm_ref):
    L = CHUNK
    nl = len(HGRN_LEVELS)
    n = len(heads)

    fs, kfs, qs, bs = [], [], [], []
    for qraw, fraw, v, graw, lb, st in heads:
        f = lb + (1.0 - lb) * _sigmoid_rel(fraw)
        hi, mid, lo = _split3(jnp.log(f))
        fs.append(f)
        kfs.append(1.0 - f)
        qs.append(qraw * _sigmoid(qraw))
        bs.append(_dot(tril3, jnp.concatenate([hi, mid, lo], axis=0)))

    os_, atts, sts, vbs = [], [], [], []
    for i in range(n):
        qraw, fraw, v, graw, lb, st = heads[i]
        q, kf, b = qs[i], kfs[i], bs[i]
        vb = v.astype(BF16)
        o = _dot_nt((q * jnp.exp(b)).astype(BF16), st.astype(BF16))
        att = jnp.zeros((L, L), F32)
        for li, sh in enumerate(HGRN_LEVELS):
            m = 1 << sh
            ref_rows = [jnp.broadcast_to(b[(2 * j + 1) * m - 1:(2 * j + 1) * m, :], (2 * m, B_DK))
                        for j in range(L // (2 * m))]
            ref = ref_rows[0] if len(ref_rows) == 1 else jnp.concatenate(ref_rows, axis=0)
            w = jnp.exp((b - ref) * rowm_ref[li])
            wq = w * rowm_ref[nl + li]
            att = att + sqm_ref[li] * _dot_nt((q * wq).astype(BF16), (kf * (w - wq)).astype(BF16))
        b_last = b[L - 1:L, :]
        kdec = (kf * jnp.exp(b_last - b)).astype(BF16)
        sts.append(st * jnp.exp(b_last) + _dot_tn(vb, kdec))
        os_.append(o)
        atts.append(att)
        vbs.append(vb)

    outs = []
    for i in range(n):
        q, f, att = qs[i], fs[i], atts[i]
        hd = kfs[i]
        for d in range(HGRN_DIAG):
            if d > 0:
                hd = f * pltpu.roll(hd, 1, 0)
            att = att + sqm_ref[nl + d] * jnp.sum(q * hd, axis=-1, keepdims=True)
        outs.append(os_[i] + _dot(att.astype(BF16), vbs[i]))

    res = []
    for i in range(n):
        o, graw = outs[i], heads[i][3]
        ms = jnp.mean(o * o, axis=-1, keepdims=True)
        res.append((o * lax.rsqrt(ms + EPS) * gn * (graw * _sigmoid(graw)), sts[i]))
    return res


def _hgrn_kernel(*refs, nchunk, has_s0):
    if has_s0:
        q_ref, f_ref, v_ref, g_ref, lb_ref, gn_ref, tril_ref, rowm_ref, sqm_ref, s0_ref, y_ref, sf_ref, st = refs
    else:
        q_ref, f_ref, v_ref, g_ref, lb_ref, gn_ref, tril_ref, rowm_ref, sqm_ref, y_ref, sf_ref, st = refs
    c = pl.program_id(1)

    @pl.when(c == 0)
    def _():
        for h in range(B_HEADS):
            st[h] = s0_ref[h].T if has_s0 else jnp.zeros((LANE, B_DK), F32)

    gn = gn_ref[...]
    tril3 = tril_ref[...]

    def body(n, carry):
        sl = pl.ds(pl.multiple_of(n * CHUNK, CHUNK), CHUNK)
        hsl = [slice(h * LANE, (h + 1) * LANE) for h in range(B_HEADS)]
        heads = [(q_ref[sl, hs], f_ref[sl, hs], v_ref[sl, hs], g_ref[sl, hs], lb_ref[:, hs], st[h])
                 for h, hs in enumerate(hsl)]
        for h, (y, st_new) in enumerate(_hgrn_chunk(heads, gn, tril3, rowm_ref, sqm_ref)):
            st[h] = st_new
            y_ref[sl, hsl[h]] = y.astype(BF16)
        return carry

    lax.fori_loop(0, nchunk, body, 0, unroll=min(2, nchunk))

    @pl.when(c == pl.num_programs(1) - 1)
    def _():
        for h in range(B_HEADS):
            sf_ref[h] = st[h].T


def _hgrn(braw, lb, gn, s0, layer, tb):
    bsz, t, _ = braw.shape
    has_s0 = s0 is not None
    kern = functools.partial(_hgrn_kernel, nchunk=tb // CHUNK, has_s0=has_s0)
    width = B_HEADS * LANE
    tril3, rowm, sqm = _hgrn_tables()

    def col(j):
        return pl.BlockSpec((None, tb, width), lambda b, c: (b, c, j))

    def const(shape):
        return pl.BlockSpec(shape, lambda b, c: (0,) * len(shape))

    in_specs = [col(0), col(1), col(2), col(3), const((1, width)), const((1, LANE)),
                const(tril3.shape), const(rowm.shape), const(sqm.shape)]
    args = [braw, braw, braw, braw, lb, gn, tril3, rowm, sqm]
    if has_s0:
        in_specs.append(pl.BlockSpec((None, None, B_HEADS, B_DK, LANE), lambda b, c: (layer, b, 0, 0, 0)))
        args.append(s0)
    return pl.pallas_call(
        kern,
        grid=(bsz, t // tb),
        in_specs=in_specs,
        out_specs=[
            pl.BlockSpec((None, tb, width), lambda b, c: (b, c, 0)),
            pl.BlockSpec((None, B_HEADS, B_DK, LANE), lambda b, c: (b, 0, 0, 0)),
        ],
        out_shape=[
            jax.ShapeDtypeStruct((bsz, t, width), BF16),
            jax.ShapeDtypeStruct((bsz, B_HEADS, B_DK, LANE), F32),
        ],
        scratch_shapes=[pltpu.VMEM((B_HEADS, LANE, B_DK), F32)],
        compiler_params=_cparams(("parallel", "arbitrary")),
        name="hgrn",
    )(*args)


S5_COLS = 512


def _s5_kernel(u_ref, bm_ref, cm_ref, a_ref, d_ref, h0_ref, y_ref, hf_ref, xs, hs, *, tc, nb):
    step_id = pl.program_id(0)

    @pl.when(step_id == 0)
    def _():
        hs[...] = h0_ref[...]

    u = u_ref[...]
    xs[...] = _dot(u.astype(BF16), bm_ref[...])
    for cc in range(C_LANES // S5_COLS):
        re = slice(cc * S5_COLS, (cc + 1) * S5_COLS)
        im = slice(C_LANES + cc * S5_COLS, C_LANES + (cc + 1) * S5_COLS)
        ar = a_ref[0:nb, re]
        ai = a_ref[nb:2 * nb, re]

        def step(t, carry):
            hr, hi = carry
            rows = pl.ds(pl.multiple_of(t * nb, nb), nb)
            nr = ar * hr - ai * hi + xs[rows, re]
            ni = ar * hi + ai * hr + xs[rows, im]
            xs[rows, re] = nr
            xs[rows, im] = ni
            return nr, ni

        hr, hi = lax.fori_loop(0, tc, step, (hs[:, re], hs[:, im]), unroll=8)
        hs[:, re] = hr
        hs[:, im] = hi
    y = _dot(xs[...].astype(BF16), cm_ref[...]) + d_ref[...] * u
    y_ref[...] = y.astype(BF16)
    hf_ref[...] = hs[...]


def _s5(u_tm, bm, cm, a, d, h0, tc):
    n, _ = u_tm.shape
    nb = h0.shape[0]
    rows = tc * nb
    kern = functools.partial(_s5_kernel, tc=tc, nb=nb)
    const = lambda i: (0, 0)
    return pl.pallas_call(
        kern,
        grid=(n // rows,),
        in_specs=[
            pl.BlockSpec((rows, C_WIDTH), lambda i: (i, 0)),
            pl.BlockSpec((C_WIDTH, 2 * C_LANES), const, pipeline_mode=pl.Buffered(1)),
            pl.BlockSpec((2 * C_LANES, C_WIDTH), const, pipeline_mode=pl.Buffered(1)),
            pl.BlockSpec((2 * nb, C_LANES), const),
            pl.BlockSpec((1, C_WIDTH), const),
            pl.BlockSpec((nb, 2 * C_LANES), const),
        ],
        out_specs=[
            pl.BlockSpec((rows, C_WIDTH), lambda i: (i, 0)),
            pl.BlockSpec((nb, 2 * C_LANES), const),
        ],
        out_shape=[
            jax.ShapeDtypeStruct((n, C_WIDTH), BF16),
            jax.ShapeDtypeStruct((nb, 2 * C_LANES), F32),
        ],
        scratch_shapes=[pltpu.VMEM((rows, 2 * C_LANES), F32), pltpu.VMEM((nb, 2 * C_LANES), F32)],
        compiler_params=_cparams(("arbitrary",)),
        name="s5",
    )(u_tm, bm, cm, a, d, h0)


def _s5_tables(a_re, a_im, b_re, b_im, c_re, c_im, log_dt, nb):
    dt = jnp.exp(log_dt)[:, None]
    mag = jnp.exp(a_re * dt)
    abr, abi = mag * jnp.cos(a_im * dt), mag * jnp.sin(a_im * dt)
    den = a_re * a_re + a_im * a_im
    coef_r = ((abr - 1.0) * a_re + abi * a_im) / den
    coef_i = (abi * a_re - (abr - 1.0) * a_im) / den
    fr = coef_r[..., None] * b_re - coef_i[..., None] * b_im
    fi = coef_r[..., None] * b_im + coef_i[..., None] * b_re
    eye = jnp.eye(C_NGROUPS, dtype=F32)

    def blockdiag_in(m):
        return jnp.einsum('gpj,gh->gjhp', m, eye).reshape(C_WIDTH, C_LANES)

    def blockdiag_out(m):
        return jnp.einsum('gjp,gh->gphj', m, eye).reshape(C_LANES, C_WIDTH)

    bm = jnp.concatenate([blockdiag_in(fr), blockdiag_in(fi)], axis=1).astype(BF16)
    cm = jnp.concatenate([blockdiag_out(c_re), -blockdiag_out(c_im)], axis=0).astype(BF16)
    a = jnp.concatenate([jnp.broadcast_to(abr.reshape(1, C_LANES), (nb, C_LANES)),
                         jnp.broadcast_to(abi.reshape(1, C_LANES), (nb, C_LANES))], axis=0)
    return bm, cm, a


def _pair_max_sum(v):
    best = v[0:1] + v[1:2]
    for i, j in ((0, 2), (0, 3), (1, 2), (1, 3), (2, 3)):
        best = jnp.maximum(best, v[i:i + 1] + v[j:j + 1])
    return best


def _route(logits_t, rb):
    mx = jnp.max(logits_t, axis=0, keepdims=True)
    ex = jnp.exp(logits_t - mx)
    s = ex / jnp.sum(ex, axis=0, keepdims=True)
    sb = s + rb
    g = EXPERTS_PER_GROUP
    best = _pair_max_sum(sb[0:g])
    sel = jnp.zeros(best.shape, jnp.int32)
    for gi in range(1, N_EXPERT_GROUPS):
        sc = _pair_max_sum(sb[gi * g:(gi + 1) * g])
        take = sc > best
        best = jnp.where(take, sc, best)
        sel = jnp.where(take, gi, sel)
    sb4 = jnp.zeros((g,) + best.shape[1:], F32)
    s4 = jnp.zeros((g,) + best.shape[1:], F32)
    for gi in range(N_EXPERT_GROUPS):
        on = sel == gi
        sb4 = jnp.where(on, sb[gi * g:(gi + 1) * g], sb4)
        s4 = jnp.where(on, s[gi * g:(gi + 1) * g], s4)
    rows = []
    for e in range(g):
        rank = jnp.zeros(best.shape, jnp.int32)
        for j in range(g):
            if j == e:
                continue
            ahead = (sb4[j:j + 1] > sb4[e:e + 1]) if j > e else (sb4[j:j + 1] >= sb4[e:e + 1])
            rank = rank + ahead.astype(jnp.int32)
        rows.append(jnp.where(rank < 2, s4[e:e + 1], 0.0))
    w4 = jnp.concatenate(rows, axis=0)
    w4 = w4 / jnp.sum(w4, axis=0, keepdims=True)
    return jnp.concatenate([jnp.where(sel == gi, w4, 0.0) for gi in range(N_EXPERT_GROUPS)], axis=0)


def _merge_kernel(x_ref, ya_ref, yb_ref, ys_ref, gt_ref, mod_ref, g2_ref,
                  wglu_ref, wa_ref, wb_ref, wc_ref, wout_ref, rw_ref, rb_ref,
                  x1_ref, h2_ref, gates_ref):
    d = D_MODEL
    z = _dot(ys_ref[...], wglu_ref[...])
    yc = (z[:, :C_WIDTH] * _sigmoid(z[:, C_WIDTH:])).astype(BF16)
    merged = (_sigmoid(gt_ref[:, 0:d]) * _dot(ya_ref[...], wa_ref[...])
              + _sigmoid(gt_ref[:, d:2 * d]) * _dot(yb_ref[...], wb_ref[...])
              + _sigmoid(gt_ref[:, 2 * d:3 * d]) * _dot(yc, wc_ref[...]))
    mix = _dot(merged.astype(BF16), wout_ref[...])
    x1 = x_ref[...] + mod_ref[:, 2 * d:3 * d] * mix
    x1_ref[...] = x1
    ms = jnp.mean(x1 * x1, axis=-1, keepdims=True)
    h2 = (x1 * lax.rsqrt(ms + EPS) * g2_ref[...]) * (1.0 + mod_ref[:, 4 * d:5 * d]) + mod_ref[:, 3 * d:4 * d]
    h_hi, h_mid, h_lo = _split3(h2)
    h2_ref[...] = h_hi
    hcat = jnp.concatenate([h_hi, h_mid, h_hi, h_lo, h_mid, h_hi], axis=1)
    gates_t = _route(_dot_nt(rw_ref[...], hcat), rb_ref[...])
    pad = jnp.zeros((LANE - N_EXPERTS, gates_t.shape[1]), F32)
    gates_ref[...] = jnp.concatenate([gates_t, pad], axis=0).T


def _merge(x, ya, yb, ys_tm, graw, mod, g2, wglu, wa, wb, wc, wout, rw_t, rb, tm):
    bsz, t, d = x.shape
    row = lambda b, i: (b, i, 0)
    const = lambda b, i: (0, 0)

    def wspec(shape):
        return pl.BlockSpec(shape, const, pipeline_mode=pl.Buffered(1))

    return pl.pallas_call(
        _merge_kernel,
        grid=(bsz, t // tm),
        in_specs=[
            pl.BlockSpec((None, tm, d), row),
            pl.BlockSpec((None, tm, 512), row),
            pl.BlockSpec((None, tm, 512), row),
            pl.BlockSpec((tm, C_WIDTH), lambda b, i: (i, b)),
            pl.BlockSpec((None, tm, 3 * d), row),
            pl.BlockSpec((None, 1, 6 * d), lambda b, i: (b, 0, 0)),
            pl.BlockSpec((1, d), const),
            wspec((C_WIDTH, 2 * C_WIDTH)), wspec((512, d)), wspec((512, d)), wspec((C_WIDTH, d)),
            wspec((d, d)), wspec((N_EXPERTS, 6 * d)), wspec((N_EXPERTS, 1)),
        ],
        out_specs=[
            pl.BlockSpec((None, tm, d), row),
            pl.BlockSpec((None, tm, d), row),
            pl.BlockSpec((None, tm, LANE), row),
        ],
        out_shape=[
            jax.ShapeDtypeStruct((bsz, t, d), F32),
            jax.ShapeDtypeStruct((bsz, t, d), BF16),
            jax.ShapeDtypeStruct((bsz, t, LANE), F32),
        ],
        compiler_params=_cparams(("parallel", "parallel")),
        name="merge_router",
    )(x, ya, yb, ys_tm, graw, mod, g2, wglu, wa, wb, wc, wout, rw_t, rb)


def _moe_kernel(h_ref, gates_ref, x1_ref, mod_ref, wg_ref, wu_ref, wd_ref, gf_ref, o_ref, acc, *, final):
    e = pl.program_id(1)
    nbt, tb, d = h_ref.shape
    tm = nbt * tb

    @pl.when(e == 0)
    def _():
        acc[...] = jnp.zeros_like(acc)

    h = h_ref[...].reshape(tm, d)
    a = _dot(h, wg_ref[...].astype(BF16))
    u = _dot(h, wu_ref[...].astype(BF16))
    gates = gates_ref[...].reshape(tm, LANE)
    lane = lax.broadcasted_iota(jnp.int32, gates.shape, 1)
    gcol = jnp.sum(jnp.where(lane == e, gates, 0.0), axis=-1, keepdims=True)
    hid = (a * _sigmoid(a)) * u * gcol
    acc[...] += _dot(hid.astype(BF16), wd_ref[...].astype(BF16))

    @pl.when(e == pl.num_programs(1) - 1)
    def _():
        x2 = x1_ref[...] + mod_ref[:, :, 5 * d:6 * d] * acc[...].reshape(nbt, tb, d)
        if final:
            ms = jnp.mean(x2 * x2, axis=-1, keepdims=True)
            x2 = x2 * lax.rsqrt(ms + EPS) * gf_ref[...]
        o_ref[...] = x2


def _moe(h2, gates, x1, mod, wg, wu, wd, gfinal, layer, nbt, tb, final):
    bsz, t, d = x1.shape
    kern = functools.partial(_moe_kernel, final=final)
    tok = lambda i, e: (i // (t // tb), i % (t // tb), 0) if nbt == 1 else (i, 0, 0)
    nsteps = (bsz // nbt) * (t // tb)
    return pl.pallas_call(
        kern,
        grid=(nsteps, N_EXPERTS),
        in_specs=[
            pl.BlockSpec((nbt, tb, d), tok),
            pl.BlockSpec((nbt, tb, LANE), tok),
            pl.BlockSpec((nbt, tb, d), tok),
            pl.BlockSpec((nbt, 1, 6 * d), lambda i, e: ((i // (t // tb)) if nbt == 1 else i, 0, 0)),
            pl.BlockSpec((None, None, d, EXPERT_DFF), lambda i, e: (layer, e, 0, 0)),
            pl.BlockSpec((None, None, d, EXPERT_DFF), lambda i, e: (layer, e, 0, 0)),
            pl.BlockSpec((None, None, EXPERT_DFF, d), lambda i, e: (layer, e, 0, 0)),
            pl.BlockSpec((1, d), lambda i, e: (0, 0)),
        ],
        out_specs=pl.BlockSpec((nbt, tb, d), tok),
        out_shape=jax.ShapeDtypeStruct((bsz, t, d), F32),
        scratch_shapes=[pltpu.VMEM((nbt * tb, d), F32)],
        compiler_params=_cparams(("parallel", "arbitrary")),
        name="moe",
    )(h2, gates, x1, mod, wg, wu, wd, gfinal)


def _rope_tables(pos):
    inv = ROPE_THETA ** (-jnp.arange(0, A_DH, 2, dtype=F32) / A_DH)
    ang = pos.astype(F32)[:, None] * inv[None, :]
    cos, sin = jnp.cos(ang), jnp.sin(ang)
    seg_cos = jnp.concatenate([cos, cos], axis=1)
    seg_sin = jnp.concatenate([-sin, sin], axis=1)
    reps = A_QK // A_DH
    return jnp.tile(seg_cos, (1, reps)), jnp.tile(seg_sin, (1, reps))


def _trunk(x, mods, pos, p, wts, past, tiles):
    bsz, t, d = x.shape
    depth = mods.shape[0]
    cos, sin = _rope_tables(pos)
    lb_p = jax.nn.softmax(p['hgrn_lb'].astype(F32), axis=0)
    lbs = jnp.cumsum(lb_p, axis=0) - lb_p[0:1]
    w_hi, w_mid, w_lo = _split3(p['router_w'].T.astype(F32))
    rw_t = jnp.concatenate([w_hi, w_hi, w_mid, w_hi, w_mid, w_lo], axis=1)
    rb = p['router_b'].reshape(N_EXPERTS, 1).astype(F32)
    gfinal = p['final_norm_g'].reshape(1, d)
    hs, rs, ims = [], [], []
    k_all = v_all = None
    for l in range(depth):
        mod = mods[l]
        q, k, v, k_all, v_all, braw, cu_tm, graw = _in_projection(
            x, mod, p['norm1_g'][l].reshape(1, d), wts['w_in'][l], cos, sin, tiles['tm'], l, depth, k_all, v_all)

        lam_init = 0.8 - 0.6 * math.exp(-0.3 * l)
        lp = p['diff_lambda'][l].astype(F32)
        lam = jnp.exp(jnp.sum(lp[0] * lp[1])) - jnp.exp(jnp.sum(lp[2] * lp[3])) + lam_init
        lam_row = jnp.full((1, LANE), lam, F32)
        subg = p['diff_subln_g'][l].reshape(1, A_DV)
        if past is None:
            ya = _attn_prompt(lam_row, q, k, v, subg, 1.0 - lam_init, tiles['tq'])
        else:
            pk = past[0].reshape(depth, bsz, -1, A_HEADS * 2 * A_DH)
            pv = past[1].reshape(depth, bsz, -1, A_HEADS * A_DV)
            ya = _attn_sample(lam_row, q, pk, pv, l, k, v, subg, 1.0 - lam_init)

        yb, s_h = _hgrn(braw, lbs[l].reshape(1, B_HEADS * B_DK), p['hgrn_norm_g'][l].reshape(1, LANE),
                        None if past is None else past[2], l, tiles['tb'])

        bm, cm, a_tab = _s5_tables(p['s5_a_re'][l], p['s5_a_im'][l], p['s5_b_re'][l], p['s5_b_im'][l],
                                   p['s5_c_re'][l], p['s5_c_im'][l], p['s5_log_dt'][l], bsz)
        if past is None:
            h0 = jnp.zeros((bsz, 2 * C_LANES), F32)
        else:
            h0 = jnp.concatenate([past[3][l].reshape(bsz, C_LANES), past[4][l].reshape(bsz, C_LANES)], axis=1)
        ys_tm, s5_state = _s5(cu_tm.reshape(t * bsz, C_WIDTH), bm, cm, a_tab,
                              p['s5_d'][l].reshape(1, C_WIDTH), h0, tiles['tc'])

        x1, h2, gates = _merge(x, ya, yb, ys_tm.reshape(t, bsz * C_WIDTH), graw, mod,
                               p['norm2_g'][l].reshape(1, d), wts['w_glu'][l], wts['w_a'][l], wts['w_b'][l],
                               wts['w_c'][l], wts['w_out'][l], rw_t, rb, tiles['tm'])
        x = _moe(h2, gates, x1, mod, p['moe_w_gate'], p['moe_w_up'], p['moe_w_down'], gfinal, l,
                 tiles['moe_nb'], tiles['moe_tb'], final=(l == depth - 1))

        hs.append(s_h)
        rs.append(s5_state[:, :C_LANES].reshape(bsz, C_NGROUPS, C_STATE))
        ims.append(s5_state[:, C_LANES:].reshape(bsz, C_NGROUPS, C_STATE))
    return (x, k_all, v_all, jnp.stack(hs), jnp.stack(rs), jnp.stack(ims))


def kernel(x_prompt, x_sample, cache_k, cache_v, state_hgrn, state_s5_re, state_s5_im, c_prompt, c_sample, w_mod, b_mod, norm1_g, norm2_g, w_in, diff_lambda, diff_subln_g, hgrn_lb, hgrn_norm_g, s5_a_re, s5_a_im, s5_b_re, s5_b_im, s5_c_re, s5_c_im, s5_d, s5_log_dt, s5_w_glu, w_branch_a, w_branch_b, w_branch_c, w_out, router_w, router_b, moe_w_gate, moe_w_up, moe_w_down, final_norm_g):
    p = {
        'norm1_g': norm1_g, 'norm2_g': norm2_g, 'diff_lambda': diff_lambda, 'diff_subln_g': diff_subln_g,
        'hgrn_lb': hgrn_lb, 'hgrn_norm_g': hgrn_norm_g, 's5_a_re': s5_a_re, 's5_a_im': s5_a_im,
        's5_b_re': s5_b_re, 's5_b_im': s5_b_im, 's5_c_re': s5_c_re, 's5_c_im': s5_c_im, 's5_d': s5_d,
        's5_log_dt': s5_log_dt, 'router_w': router_w, 'router_b': router_b,
        'moe_w_gate': moe_w_gate, 'moe_w_up': moe_w_up, 'moe_w_down': moe_w_down, 'final_norm_g': final_norm_g,
    }
    wts = {'w_in': w_in.astype(BF16), 'w_glu': s5_w_glu.astype(BF16), 'w_a': w_branch_a.astype(BF16),
           'w_b': w_branch_b.astype(BF16), 'w_c': w_branch_c.astype(BF16), 'w_out': w_out.astype(BF16)}
    bp, tp, d = x_prompt.shape
    bs, ts, _ = x_sample.shape
    depth = w_mod.shape[0]
    mods = _modulation(jnp.concatenate([c_prompt, c_sample], axis=0), w_mod, b_mod)
    mods = mods.reshape(depth, bp + bs, 1, 6 * d)
    pos_p = jnp.arange(tp)
    pos_s = cache_k.shape[2] + jnp.arange(ts)
    tiles_p = dict(tm=min(256, tp), tq=min(256, tp), tb=min(512, tp), tc=min(64, tp),
                   moe_nb=1, moe_tb=min(1024, tp))
    tiles_s = dict(tm=ts, tq=ts, tb=ts, tc=ts, moe_nb=bs, moe_tb=ts)
    y_p, k_p, v_p, h_p, re_p, im_p = _trunk(x_prompt, mods[:, :bp], pos_p, p, wts, None, tiles_p)
    y_s, k_s, v_s, h_s, re_s, im_s = _trunk(x_sample, mods[:, bp:], pos_s, p, wts,
                                            (cache_k, cache_v, state_hgrn, state_s5_re, state_s5_im), tiles_s)
    return (y_p, y_s, k_p, v_p, h_p, re_p, im_p, k_s, v_s, h_s, re_s, im_s)
```

```python
import functools
import math

import numpy as np
import jax
import jax.numpy as jnp
from jax import lax
from jax.experimental import pallas as pl
from jax.experimental.pallas import tpu as pltpu

F32 = jnp.float32
BF16 = jnp.bfloat16

D_MODEL = 1024
CHUNK = 64
EPS = 1e-6
MASK_NEG = -1e30
ROPE_THETA = 10000.0
A_HEADS = 4
A_DH = 64
A_DV = 128
A_QK = 512
B_HEADS = 4
B_DK = 128
C_WIDTH = 512
C_GROUP = 16
C_NGROUPS = 32
C_STATE = 64
C_LANES = C_NGROUPS * C_STATE
N_EXPERTS = 16
N_EXPERT_GROUPS = 4
EXPERTS_PER_GROUP = 4
EXPERT_DFF = 512
IN_COLS = 7168
COL_AK, COL_AV, COL_B, COL_CU, COL_G = 512, 1024, 1536, 3584, 4096

LANE = 128
LOG2E = math.log2(math.e)
ONES_ROWS = 16
VMEM_LIMIT = 56 * 1024 * 1024


def _cparams(sem):
    return pltpu.CompilerParams(dimension_semantics=sem, vmem_limit_bytes=VMEM_LIMIT)


def _sigmoid(x):
    return 0.5 * jnp.tanh(0.5 * x) + 0.5


def _sigmoid_rel(x):
    return jax.nn.sigmoid(x)


def _dot(a, b):
    return jnp.dot(a, b, preferred_element_type=F32)


def _dot_nt(a, b):
    return lax.dot_general(a, b, (((1,), (1,)), ((), ())), preferred_element_type=F32)


def _dot_tn(a, b):
    return lax.dot_general(a, b, (((0,), (0,)), ((), ())), preferred_element_type=F32)


def _mod_kernel(c_ref, w_ref, b_ref, o_ref):
    c = c_ref[...]
    a = (c * _sigmoid(c)).astype(BF16)
    o_ref[...] = _dot(a, w_ref[...].astype(BF16)) + b_ref[...]


def _modulation(c_all, w_mod, b_mod):
    depth, d, n6 = w_mod.shape
    nb = c_all.shape[0]
    tn = 1536
    return pl.pallas_call(
        _mod_kernel,
        grid=(depth, n6 // tn),
        in_specs=[
            pl.BlockSpec((nb, d), lambda l, j: (0, 0)),
            pl.BlockSpec((None, d, tn), lambda l, j: (l, 0, j)),
            pl.BlockSpec((None, 1, tn), lambda l, j: (l, 0, j)),
        ],
        out_specs=pl.BlockSpec((None, nb, tn), lambda l, j: (l, 0, j)),
        out_shape=jax.ShapeDtypeStruct((depth, nb, n6), F32),
        compiler_params=_cparams(("parallel", "parallel")),
        name="modulation",
    )(c_all, w_mod, b_mod.reshape(depth, 1, n6))


def _swap_half(a):
    parts = []
    for j in range(a.shape[1] // LANE):
        s = a[:, LANE * j:LANE * (j + 1)]
        lane = lax.broadcasted_iota(jnp.int32, s.shape, 1)
        first = (lane & (A_DH - 1)) < (A_DH // 2)
        parts.append(jnp.where(first, pltpu.roll(s, LANE - A_DH // 2, 1), pltpu.roll(s, A_DH // 2, 1)))
    return jnp.concatenate(parts, axis=1)


def _in_kernel(x_ref, mod_ref, g_ref, w_ref, cos_ref, sin_ref, k_all_ref, v_all_ref,
               q_ref, kb_ref, vb_ref, k4_ref, v4_ref, b_ref, cu_ref, gt_ref):
    del k_all_ref, v_all_ref
    x = x_ref[...]
    ms = jnp.mean(x * x, axis=-1, keepdims=True)
    y = x * lax.rsqrt(ms + EPS) * g_ref[...]
    h = y * (1.0 + mod_ref[:, D_MODEL:2 * D_MODEL]) + mod_ref[:, 0:D_MODEL]
    hb = h.astype(BF16)

    def proj(c0, width):
        return _dot(hb, w_ref[:, c0:c0 + width])

    cos = cos_ref[...]
    sin = sin_ref[...]

    def rope(a):
        return a * cos + _swap_half(a) * sin

    q_ref[...] = (rope(proj(0, A_QK)) * (A_DH ** -0.5 * LOG2E)).astype(BF16)
    k = rope(proj(COL_AK, A_QK))
    v = proj(COL_AV, 512)
    kb_ref[...] = k.astype(BF16)
    vb_ref[...] = v.astype(BF16)
    for h in range(A_HEADS):
        k4_ref[:, h, :] = k[:, h * LANE:(h + 1) * LANE]
        v4_ref[:, h, :] = v[:, h * LANE:(h + 1) * LANE]
    for j in range(4):
        b_ref[:, 512 * j:512 * (j + 1)] = proj(COL_B + 512 * j, 512)
    cu_ref[...] = proj(COL_CU, C_WIDTH)
    for j in range(6):
        gt_ref[:, 512 * j:512 * (j + 1)] = proj(COL_G + 512 * j, 512)


def _in_projection(x, mod, g1, w_in, cos, sin, tm, layer, depth, k_all, v_all):
    bsz, t, d = x.shape
    nt = t // tm
    row = lambda b, i: (b, i, 0)
    cache_spec = pl.BlockSpec((None, None, tm, A_HEADS, LANE), lambda b, i: (layer, b, i, 0, 0))
    cache_shape = jax.ShapeDtypeStruct((depth, bsz, t, A_HEADS, LANE), F32)
    in_specs = [
        pl.BlockSpec((None, tm, d), row),
        pl.BlockSpec((None, 1, 6 * d), lambda b, i: (b, 0, 0)),
        pl.BlockSpec((1, d), lambda b, i: (0, 0)),
        pl.BlockSpec((d, IN_COLS), lambda b, i: (0, 0), pipeline_mode=pl.Buffered(1)),
        pl.BlockSpec((tm, A_QK), lambda b, i: (i, 0)),
        pl.BlockSpec((tm, A_QK), lambda b, i: (i, 0)),
        pl.BlockSpec(memory_space=pl.ANY),
        pl.BlockSpec(memory_space=pl.ANY),
    ]
    args = [x, mod, g1, w_in, cos, sin, k_all, v_all]
    return pl.pallas_call(
        _in_kernel,
        grid=(bsz, nt),
        in_specs=in_specs,
        out_specs=[
            pl.BlockSpec((None, tm, A_QK), row),
            pl.BlockSpec((None, tm, A_QK), row),
            pl.BlockSpec((None, tm, 512), row),
            cache_spec,
            cache_spec,
            pl.BlockSpec((None, tm, 2048), row),
            pl.BlockSpec((tm, C_WIDTH), lambda b, i: (i, b)),
            pl.BlockSpec((None, tm, 3 * d), row),
        ],
        out_shape=[
            jax.ShapeDtypeStruct((bsz, t, A_QK), BF16),
            jax.ShapeDtypeStruct((bsz, t, A_QK), BF16),
            jax.ShapeDtypeStruct((bsz, t, 512), BF16),
            cache_shape,
            cache_shape,
            jax.ShapeDtypeStruct((bsz, t, 2048), F32),
            jax.ShapeDtypeStruct((t, bsz * C_WIDTH), F32),
            jax.ShapeDtypeStruct((bsz, t, 3 * d), F32),
        ],
        input_output_aliases={6: 3, 7: 4},
        compiler_params=_cparams(("parallel", "parallel")),
        name="in_projection",
    )(*args)


def _split_maps(q):
    lane = lax.broadcasted_iota(jnp.int32, q.shape, 1)
    zero = jnp.zeros_like(q)
    return jnp.concatenate([jnp.where(lane < A_DH, q, zero), jnp.where(lane >= A_DH, q, zero)], axis=0)


def _attn_finish(acc, l, lam, g, scale_out, tq):
    o = acc[:tq] / l[:tq] - lam * (acc[tq:] / l[tq:])
    ms = jnp.mean(o * o, axis=-1, keepdims=True)
    return (o * lax.rsqrt(ms + EPS) * g) * scale_out


def _attn_prompt_kernel(lam_ref, q_ref, k_ref, v_ref, g_ref, o_ref, vt, *, tq, scale_out):
    i = pl.program_id(1)
    nkv = vt.shape[0] // A_HEADS

    @pl.when(i == 0)
    def _():
        for h in range(A_HEADS):
            for jj in range(nkv):
                blk = (slice(jj * tq, (jj + 1) * tq), slice(h * LANE, (h + 1) * LANE))
                vt[h * nkv + jj] = jnp.concatenate(
                    [v_ref[blk].astype(F32).T.astype(BF16), jnp.ones((ONES_ROWS, tq), BF16)], axis=0)

    qq = [_split_maps(q_ref[:, h * LANE:(h + 1) * LANE]) for h in range(A_HEADS)]

    def scores(j, h):
        ks = k_ref[pl.ds(pl.multiple_of(j * tq, tq), tq), h * LANE:(h + 1) * LANE]
        return _dot_nt(ks, qq[h])

    def tile(j, carry, s, h, masked):
        m, acc = carry
        if masked:
            kr = lax.broadcasted_iota(jnp.int32, s.shape, 0)
            qc = lax.broadcasted_iota(jnp.int32, s.shape, 1)
            s = jnp.where((kr >> 6) <= ((qc & (tq - 1)) >> 6), s, MASK_NEG)
        mn = jnp.maximum(m, jnp.max(s, axis=0, keepdims=True))
        p = jnp.exp2((s - mn).astype(BF16))
        acc = jnp.exp2(m - mn) * acc + _dot(vt[h * nkv + j], p)
        return mn, acc

    def tiles(j, carries, masked):
        ss = [scores(j, h) for h in range(A_HEADS)]
        return tuple(tile(j, carries[h], ss[h], h, masked) for h in range(A_HEADS))

    init = (jnp.full((1, 2 * tq), MASK_NEG, F32), jnp.zeros((A_DV + ONES_ROWS, 2 * tq), F32))
    carries = lax.fori_loop(0, i, lambda j, c: tiles(j, c, False), (init,) * A_HEADS)
    carries = tiles(i, carries, True)
    for h in range(A_HEADS):
        acc = carries[h][1]
        acc = acc[:A_DV] / acc[A_DV:A_DV + 1]
        o = acc[:, :tq] - lam_ref[0:1, 0:1] * acc[:, tq:]
        ms = jnp.mean(o * o, axis=0, keepdims=True)
        y = (o * lax.rsqrt(ms + EPS) * g_ref[...]) * scale_out
        o_ref[:, h * LANE:(h + 1) * LANE] = y.T.astype(BF16)


def _attn_prompt(lam, q, k, v, g, scale_out, tq):
    bsz, t, _ = q.shape
    assert tq % CHUNK == 0 and tq & (tq - 1) == 0
    kern = functools.partial(_attn_prompt_kernel, tq=tq, scale_out=scale_out)
    return pl.pallas_call(
        kern,
        grid=(bsz, t // tq),
        in_specs=[
            pl.BlockSpec((1, LANE), lambda b, i: (0, 0)),
            pl.BlockSpec((None, tq, A_HEADS * LANE), lambda b, i: (b, i, 0)),
            pl.BlockSpec((None, t, A_HEADS * LANE), lambda b, i: (b, 0, 0)),
            pl.BlockSpec((None, t, A_HEADS * LANE), lambda b, i: (b, 0, 0)),
            pl.BlockSpec((A_DV, 1), lambda b, i: (0, 0)),
        ],
        out_specs=pl.BlockSpec((None, tq, A_HEADS * LANE), lambda b, i: (b, i, 0)),
        out_shape=jax.ShapeDtypeStruct((bsz, t, A_HEADS * A_DV), BF16),
        scratch_shapes=[pltpu.VMEM((A_HEADS * (t // tq), A_DV + ONES_ROWS, tq), BF16)],
        compiler_params=_cparams(("parallel", "arbitrary")),
        name="attn_prompt",
    )(lam, q, k, v, g.reshape(A_DV, 1))


def _attn_sample_kernel(lam_ref, q_ref, pk_ref, pv_ref, k_ref, v_ref, g_ref, o_ref, *, tq, scale_out):
    qq = _split_maps(q_ref[...])
    s_p = _dot_nt(qq, pk_ref[...].astype(BF16))
    s_n = _dot_nt(qq, k_ref[...])
    m = jnp.maximum(jnp.max(s_p, axis=-1, keepdims=True), jnp.max(s_n, axis=-1, keepdims=True))
    p_p = jnp.exp2(s_p - m)
    p_n = jnp.exp2(s_n - m)
    l = jnp.sum(p_p, axis=-1, keepdims=True) + jnp.sum(p_n, axis=-1, keepdims=True)
    acc = _dot(p_p.astype(BF16), pv_ref[...].astype(BF16)) + _dot(p_n.astype(BF16), v_ref[...])
    o_ref[...] = _attn_finish(acc, l, lam_ref[...], g_ref[...], scale_out, tq).astype(BF16)


def _attn_sample(lam, q, cache_k, cache_v, layer, k, v, g, scale_out):
    bsz, t, _ = q.shape
    past = cache_k.shape[2]
    kern = functools.partial(_attn_sample_kernel, tq=t, scale_out=scale_out)
    new = lambda b, h: (b, 0, h)
    old = lambda b, h: (layer, b, 0, h)
    return pl.pallas_call(
        kern,
        grid=(bsz, A_HEADS),
        in_specs=[
            pl.BlockSpec((1, LANE), lambda b, h: (0, 0)),
            pl.BlockSpec((None, t, LANE), new),
            pl.BlockSpec((None, None, past, LANE), old),
            pl.BlockSpec((None, None, past, LANE), old),
            pl.BlockSpec((None, t, LANE), new),
            pl.BlockSpec((None, t, LANE), new),
            pl.BlockSpec((1, LANE), lambda b, h: (0, 0)),
        ],
        out_specs=pl.BlockSpec((None, t, LANE), new),
        out_shape=jax.ShapeDtypeStruct((bsz, t, A_HEADS * A_DV), BF16),
        compiler_params=_cparams(("parallel", "parallel")),
        name="attn_sample",
    )(lam, q, cache_k, cache_v, k, v, g)


def _split3(x):
    hi = x.astype(BF16)
    r = x - hi.astype(F32)
    mid = r.astype(BF16)
    lo = (r - mid.astype(F32)).astype(BF16)
    return hi, mid, lo


HGRN_LEVELS = (5, 4, 3)
HGRN_DIAG = 8


def _hgrn_tables():
    L = CHUNK
    r = np.arange(L)[:, None]
    c = np.arange(L)[None, :]
    sgn, up, pair = [], [], []
    for sh in HGRN_LEVELS:
        upper = ((r >> sh) & 1) == 1
        up.append(np.broadcast_to(upper, (L, B_DK)))
        sgn.append(np.broadcast_to(np.where(upper, 1.0, -1.0), (L, B_DK)))
        pair.append((((r >> sh) & 1) == 1) & ((r >> (sh + 1)) == (c >> (sh + 1))) & (((c >> sh) & 1) == 0))
    diag = [(c == r - d) & ((r & (HGRN_DIAG - 1)) >= d) for d in range(HGRN_DIAG)]
    tril = (c <= r).astype(np.float32)
    rowm = jnp.asarray(np.stack(sgn + up).astype(np.float32))
    sqm = jnp.asarray(np.stack(pair + diag).astype(np.float32))
    tril3 = jnp.asarray(np.concatenate([tril, tril, tril], axis=1), dtype=BF16)
    return tril3, rowm, sqm


def _hgrn_chunk(heads, gn, tril3, rowm_ref, sqm_ref):
    L = CHUNK
    nl = len(HGRN_LEVELS)
    n = len(heads)

    fs, kfs, qs, bs = [], [], [], []
    for qraw, fraw, v, graw, lb, st in heads:
        f = lb + (1.0 - lb) * _sigmoid_rel(fraw)
        hi, mid, lo = _split3(jnp.log(f))
        fs.append(f)
        kfs.append(1.0 - f)
        qs.append(qraw * _sigmoid(qraw))
        bs.append(_dot(tril3, jnp.concatenate([hi, mid, lo], axis=0)))

    os_, atts, sts, vbs = [], [], [], []
    for i in range(n):
        qraw, fraw, v, graw, lb, st = heads[i]
        q, kf, b = qs[i], kfs[i], bs[i]
        vb = v.astype(BF16)
        o = _dot_nt((q * jnp.exp(b)).astype(BF16), st.astype(BF16))
        att = jnp.zeros((L, L), F32)
        for li, sh in enumerate(HGRN_LEVELS):
            m = 1 << sh
            ref_rows = [jnp.broadcast_to(b[(2 * j + 1) * m - 1:(2 * j + 1) * m, :], (2 * m, B_DK))
                        for j in range(L // (2 * m))]
            ref = ref_rows[0] if len(ref_rows) == 1 else jnp.concatenate(ref_rows, axis=0)
            w = jnp.exp((b - ref) * rowm_ref[li])
            wq = w * rowm_ref[nl + li]
            att = att + sqm_ref[li] * _dot_nt((q * wq).astype(BF16), (kf * (w - wq)).astype(BF16))
        b_last = b[L - 1:L, :]
        kdec = (kf * jnp.exp(b_last - b)).astype(BF16)
        sts.append(st * jnp.exp(b_last) + _dot_tn(vb, kdec))
        os_.append(o)
        atts.append(att)
        vbs.append(vb)

    outs = []
    for i in range(n):
        q, f, att = qs[i], fs[i], atts[i]
        hd = kfs[i]
        for d in range(HGRN_DIAG):
            if d > 0:
                hd = f * pltpu.roll(hd, 1, 0)
            att = att + sqm_ref[nl + d] * jnp.sum(q * hd, axis=-1, keepdims=True)
        outs.append(os_[i] + _dot(att.astype(BF16), vbs[i]))

    res = []
    for i in range(n):
        o, graw = outs[i], heads[i][3]
        ms = jnp.mean(o * o, axis=-1, keepdims=True)
        res.append((o * lax.rsqrt(ms + EPS) * gn * (graw * _sigmoid(graw)), sts[i]))
    return res


def _hgrn_kernel(*refs, nchunk, has_s0):
    if has_s0:
        q_ref, f_ref, v_ref, g_ref, lb_ref, gn_ref, tril_ref, rowm_ref, sqm_ref, s0_ref, y_ref, sf_ref, st = refs
    else:
        q_ref, f_ref, v_ref, g_ref, lb_ref, gn_ref, tril_ref, rowm_ref, sqm_ref, y_ref, sf_ref, st = refs
    c = pl.program_id(1)

    @pl.when(c == 0)
    def _():
        for h in range(B_HEADS):
            st[h] = s0_ref[h].T if has_s0 else jnp.zeros((LANE, B_DK), F32)

    gn = gn_ref[...]
    tril3 = tril_ref[...]

    def body(n, carry):
        sl = pl.ds(pl.multiple_of(n * CHUNK, CHUNK), CHUNK)
        hsl = [slice(h * LANE, (h + 1) * LANE) for h in range(B_HEADS)]
        heads = [(q_ref[sl, hs], f_ref[sl, hs], v_ref[sl, hs], g_ref[sl, hs], lb_ref[:, hs], st[h])
                 for h, hs in enumerate(hsl)]
        for h, (y, st_new) in enumerate(_hgrn_chunk(heads, gn, tril3, rowm_ref, sqm_ref)):
            st[h] = st_new
            y_ref[sl, hsl[h]] = y.astype(BF16)
        return carry

    lax.fori_loop(0, nchunk, body, 0, unroll=min(2, nchunk))

    @pl.when(c == pl.num_programs(1) - 1)
    def _():
        for h in range(B_HEADS):
            sf_ref[h] = st[h].T


def _hgrn(braw, lb, gn, s0, layer, tb):
    bsz, t, _ = braw.shape
    has_s0 = s0 is not None
    kern = functools.partial(_hgrn_kernel, nchunk=tb // CHUNK, has_s0=has_s0)
    width = B_HEADS * LANE
    tril3, rowm, sqm = _hgrn_tables()

    def col(j):
        return pl.BlockSpec((None, tb, width), lambda b, c: (b, c, j))

    def const(shape):
        return pl.BlockSpec(shape, lambda b, c: (0,) * len(shape))

    in_specs = [col(0), col(1), col(2), col(3), const((1, width)), const((1, LANE)),
                const(tril3.shape), const(rowm.shape), const(sqm.shape)]
    args = [braw, braw, braw, braw, lb, gn, tril3, rowm, sqm]
    if has_s0:
        in_specs.append(pl.BlockSpec((None, None, B_HEADS, B_DK, LANE), lambda b, c: (layer, b, 0, 0, 0)))
        args.append(s0)
    return pl.pallas_call(
        kern,
        grid=(bsz, t // tb),
        in_specs=in_specs,
        out_specs=[
            pl.BlockSpec((None, tb, width), lambda b, c: (b, c, 0)),
            pl.BlockSpec((None, B_HEADS, B_DK, LANE), lambda b, c: (b, 0, 0, 0)),
        ],
        out_shape=[
            jax.ShapeDtypeStruct((bsz, t, width), BF16),
            jax.ShapeDtypeStruct((bsz, B_HEADS, B_DK, LANE), F32),
        ],
        scratch_shapes=[pltpu.VMEM((B_HEADS, LANE, B_DK), F32)],
        compiler_params=_cparams(("parallel", "arbitrary")),
        name="hgrn",
    )(*args)


S5_COLS = 512


def _s5_kernel(u_ref, bm_ref, cm_ref, a_ref, d_ref, h0_ref, y_ref, hf_ref, xs, hs, *, tc, nb):
    step_id = pl.program_id(0)

    @pl.when(step_id == 0)
    def _():
        hs[...] = h0_ref[...]

    u = u_ref[...]
    xs[...] = _dot(u.astype(BF16), bm_ref[...])
    for cc in range(C_LANES // S5_COLS):
        re = slice(cc * S5_COLS, (cc + 1) * S5_COLS)
        im = slice(C_LANES + cc * S5_COLS, C_LANES + (cc + 1) * S5_COLS)
        ar = a_ref[0:nb, re]
        ai = a_ref[nb:2 * nb, re]

        def step(t, carry):
            hr, hi = carry
            rows = pl.ds(pl.multiple_of(t * nb, nb), nb)
            nr = ar * hr - ai * hi + xs[rows, re]
            ni = ar * hi + ai * hr + xs[rows, im]
            xs[rows, re] = nr
            xs[rows, im] = ni
            return nr, ni

        hr, hi = lax.fori_loop(0, tc, step, (hs[:, re], hs[:, im]), unroll=8)
        hs[:, re] = hr
        hs[:, im] = hi
    y = _dot(xs[...].astype(BF16), cm_ref[...]) + d_ref[...] * u
    y_ref[...] = y.astype(BF16)
    hf_ref[...] = hs[...]


def _s5(u_tm, bm, cm, a, d, h0, tc):
    n, _ = u_tm.shape
    nb = h0.shape[0]
    rows = tc * nb
    kern = functools.partial(_s5_kernel, tc=tc, nb=nb)
    const = lambda i: (0, 0)
    return pl.pallas_call(
        kern,
        grid=(n // rows,),
        in_specs=[
            pl.BlockSpec((rows, C_WIDTH), lambda i: (i, 0)),
            pl.BlockSpec((C_WIDTH, 2 * C_LANES), const, pipeline_mode=pl.Buffered(1)),
            pl.BlockSpec((2 * C_LANES, C_WIDTH), const, pipeline_mode=pl.Buffered(1)),
            pl.BlockSpec((2 * nb, C_LANES), const),
            pl.BlockSpec((1, C_WIDTH), const),
            pl.BlockSpec((nb, 2 * C_LANES), const),
        ],
        out_specs=[
            pl.BlockSpec((rows, C_WIDTH), lambda i: (i, 0)),
            pl.BlockSpec((nb, 2 * C_LANES), const),
        ],
        out_shape=[
            jax.ShapeDtypeStruct((n, C_WIDTH), BF16),
            jax.ShapeDtypeStruct((nb, 2 * C_LANES), F32),
        ],
        scratch_shapes=[pltpu.VMEM((rows, 2 * C_LANES), F32), pltpu.VMEM((nb, 2 * C_LANES), F32)],
        compiler_params=_cparams(("arbitrary",)),
        name="s5",
    )(u_tm, bm, cm, a, d, h0)


def _s5_tables(a_re, a_im, b_re, b_im, c_re, c_im, log_dt, nb):
    dt = jnp.exp(log_dt)[:, None]
    mag = jnp.exp(a_re * dt)
    abr, abi = mag * jnp.cos(a_im * dt), mag * jnp.sin(a_im * dt)
    den = a_re * a_re + a_im * a_im
    coef_r = ((abr - 1.0) * a_re + abi * a_im) / den
    coef_i = (abi * a_re - (abr - 1.0) * a_im) / den
    fr = coef_r[..., None] * b_re - coef_i[..., None] * b_im
    fi = coef_r[..., None] * b_im + coef_i[..., None] * b_re
    eye = jnp.eye(C_NGROUPS, dtype=F32)

    def blockdiag_in(m):
        return jnp.einsum('gpj,gh->gjhp', m, eye).reshape(C_WIDTH, C_LANES)

    def blockdiag_out(m):
        return jnp.einsum('gjp,gh->gphj', m, eye).reshape(C_LANES, C_WIDTH)

    bm = jnp.concatenate([blockdiag_in(fr), blockdiag_in(fi)], axis=1).astype(BF16)
    cm = jnp.concatenate([blockdiag_out(c_re), -blockdiag_out(c_im)], axis=0).astype(BF16)
    a = jnp.concatenate([jnp.broadcast_to(abr.reshape(1, C_LANES), (nb, C_LANES)),
                         jnp.broadcast_to(abi.reshape(1, C_LANES), (nb, C_LANES))], axis=0)
    return bm, cm, a


def _pair_max_sum(v):
    best = v[0:1] + v[1:2]
    for i, j in ((0, 2), (0, 3), (1, 2), (1, 3), (2, 3)):
        best = jnp.maximum(best, v[i:i + 1] + v[j:j + 1])
    return best


def _route(logits_t, rb):
    mx = jnp.max(logits_t, axis=0, keepdims=True)
    ex = jnp.exp(logits_t - mx)
    s = ex / jnp.sum(ex, axis=0, keepdims=True)
    sb = s + rb
    g = EXPERTS_PER_GROUP
    best = _pair_max_sum(sb[0:g])
    sel = jnp.zeros(best.shape, jnp.int32)
    for gi in range(1, N_EXPERT_GROUPS):
        sc = _pair_max_sum(sb[gi * g:(gi + 1) * g])
        take = sc > best
        best = jnp.where(take, sc, best)
        sel = jnp.where(take, gi, sel)
    sb4 = jnp.zeros((g,) + best.shape[1:], F32)
    s4 = jnp.zeros((g,) + best.shape[1:], F32)
    for gi in range(N_EXPERT_GROUPS):
        on = sel == gi
        sb4 = jnp.where(on, sb[gi * g:(gi + 1) * g], sb4)
        s4 = jnp.where(on, s[gi * g:(gi + 1) * g], s4)
    rows, keeps = [], []
    for e in range(g):
        rank = jnp.zeros(best.shape, jnp.int32)
        for j in range(g):
            if j == e:
                continue
            ahead = (sb4[j:j + 1] > sb4[e:e + 1]) if j > e else (sb4[j:j + 1] >= sb4[e:e + 1])
            rank = rank + ahead.astype(jnp.int32)
        rows.append(jnp.where(rank < 2, s4[e:e + 1], 0.0))
        keeps.append(jnp.where(rank < 2, 1.0, 0.0))
    w4 = jnp.concatenate(rows, axis=0)
    w4 = w4 / jnp.sum(w4, axis=0, keepdims=True)
    keep4 = jnp.concatenate(keeps, axis=0)
    gates = jnp.concatenate([jnp.where(sel == gi, w4, 0.0) for gi in range(N_EXPERT_GROUPS)], axis=0)
    chosen = jnp.concatenate([jnp.where(sel == gi, keep4, 0.0) for gi in range(N_EXPERT_GROUPS)], axis=0) > 0.5
    eid = lax.broadcasted_iota(jnp.int32, gates.shape, 0).astype(F32)
    e_lo = jnp.min(jnp.where(chosen, eid, float(N_EXPERTS)), axis=0, keepdims=True)
    e_hi = jnp.max(jnp.where(chosen, eid, -1.0), axis=0, keepdims=True)
    w_lo = jnp.sum(jnp.where(eid == e_lo, gates, 0.0), axis=0, keepdims=True)
    w_hi = jnp.sum(jnp.where(eid == e_hi, gates, 0.0), axis=0, keepdims=True)
    return gates, jnp.concatenate([e_lo, e_hi, w_lo, w_hi], axis=0)


def _merge_kernel(x_ref, ya_ref, yb_ref, ys_ref, gt_ref, mod_ref, g2_ref,
                  wglu_ref, wa_ref, wb_ref, wc_ref, wout_ref, rw_ref, rb_ref,
                  x1_ref, h2_ref, gates_ref):
    d = D_MODEL
    z = _dot(ys_ref[...], wglu_ref[...])
    yc = (z[:, :C_WIDTH] * _sigmoid(z[:, C_WIDTH:])).astype(BF16)
    merged = (_sigmoid(gt_ref[:, 0:d]) * _dot(ya_ref[...], wa_ref[...])
              + _sigmoid(gt_ref[:, d:2 * d]) * _dot(yb_ref[...], wb_ref[...])
              + _sigmoid(gt_ref[:, 2 * d:3 * d]) * _dot(yc, wc_ref[...]))
    mix = _dot(merged.astype(BF16), wout_ref[...])
    x1 = x_ref[...] + mod_ref[:, 2 * d:3 * d] * mix
    x1_ref[...] = x1
    ms = jnp.mean(x1 * x1, axis=-1, keepdims=True)
    h2 = (x1 * lax.rsqrt(ms + EPS) * g2_ref[...]) * (1.0 + mod_ref[:, 4 * d:5 * d]) + mod_ref[:, 3 * d:4 * d]
    h_hi, h_mid, h_lo = _split3(h2)
    h2_ref[...] = h_hi.astype(h2_ref.dtype)
    hcat = jnp.concatenate([h_hi, h_mid, h_hi, h_lo, h_mid, h_hi], axis=1)
    gates_t, picks_t = _route(_dot_nt(rw_ref[...], hcat), rb_ref[...])
    pad = jnp.zeros((LANE - N_EXPERTS - 4, gates_t.shape[1]), F32)
    gates_ref[...] = jnp.concatenate([gates_t, picks_t, pad], axis=0).T


def _merge(x, ya, yb, ys_tm, graw, mod, g2, wglu, wa, wb, wc, wout, rw_t, rb, tm, h2_dtype):
    bsz, t, d = x.shape
    row = lambda b, i: (b, i, 0)
    const = lambda b, i: (0, 0)

    def wspec(shape):
        return pl.BlockSpec(shape, const, pipeline_mode=pl.Buffered(1))

    return pl.pallas_call(
        _merge_kernel,
        grid=(bsz, t // tm),
        in_specs=[
            pl.BlockSpec((None, tm, d), row),
            pl.BlockSpec((None, tm, 512), row),
            pl.BlockSpec((None, tm, 512), row),
            pl.BlockSpec((tm, C_WIDTH), lambda b, i: (i, b)),
            pl.BlockSpec((None, tm, 3 * d), row),
            pl.BlockSpec((None, 1, 6 * d), lambda b, i: (b, 0, 0)),
            pl.BlockSpec((1, d), const),
            wspec((C_WIDTH, 2 * C_WIDTH)), wspec((512, d)), wspec((512, d)), wspec((C_WIDTH, d)),
            wspec((d, d)), wspec((N_EXPERTS, 6 * d)), wspec((N_EXPERTS, 1)),
        ],
        out_specs=[
            pl.BlockSpec((None, tm, d), row),
            pl.BlockSpec((None, tm, d), row),
            pl.BlockSpec((None, tm, LANE), row),
        ],
        out_shape=[
            jax.ShapeDtypeStruct((bsz, t, d), F32),
            jax.ShapeDtypeStruct((bsz, t, d), h2_dtype),
            jax.ShapeDtypeStruct((bsz, t, LANE), F32),
        ],
        compiler_params=_cparams(("parallel", "parallel")),
        name="merge_router",
    )(x, ya, yb, ys_tm, graw, mod, g2, wglu, wa, wb, wc, wout, rw_t, rb)


def _moe_kernel(h_ref, gates_ref, x1_ref, mod_ref, wg_ref, wu_ref, wd_ref, gf_ref, o_ref, acc, *, final):
    e = pl.program_id(1)
    nbt, tb, d = h_ref.shape
    tm = nbt * tb

    @pl.when(e == 0)
    def _():
        acc[...] = jnp.zeros_like(acc)

    h = h_ref[...].reshape(tm, d)
    a = _dot(h, wg_ref[...].astype(BF16))
    u = _dot(h, wu_ref[...].astype(BF16))
    gates = gates_ref[...].reshape(tm, LANE)
    lane = lax.broadcasted_iota(jnp.int32, gates.shape, 1)
    gcol = jnp.sum(jnp.where(lane == e, gates, 0.0), axis=-1, keepdims=True)
    hid = (a * _sigmoid(a)) * u * gcol
    acc[...] += _dot(hid.astype(BF16), wd_ref[...].astype(BF16))

    @pl.when(e == pl.num_programs(1) - 1)
    def _():
        x2 = x1_ref[...] + mod_ref[:, :, 5 * d:6 * d] * acc[...].reshape(nbt, tb, d)
        if final:
            ms = jnp.mean(x2 * x2, axis=-1, keepdims=True)
            x2 = x2 * lax.rsqrt(ms + EPS) * gf_ref[...]
        o_ref[...] = x2


def _moe(h2, gates, x1, mod, wg, wu, wd, gfinal, layer, nbt, tb, final):
    bsz, t, d = x1.shape
    kern = functools.partial(_moe_kernel, final=final)
    tok = lambda i, e: (i // (t // tb), i % (t // tb), 0) if nbt == 1 else (i, 0, 0)
    nsteps = (bsz // nbt) * (t // tb)
    return pl.pallas_call(
        kern,
        grid=(nsteps, N_EXPERTS),
        in_specs=[
            pl.BlockSpec((nbt, tb, d), tok),
            pl.BlockSpec((nbt, tb, LANE), tok),
            pl.BlockSpec((nbt, tb, d), tok),
            pl.BlockSpec((nbt, 1, 6 * d), lambda i, e: ((i // (t // tb)) if nbt == 1 else i, 0, 0)),
            pl.BlockSpec((None, None, d, EXPERT_DFF), lambda i, e: (layer, e, 0, 0)),
            pl.BlockSpec((None, None, d, EXPERT_DFF), lambda i, e: (layer, e, 0, 0)),
            pl.BlockSpec((None, None, EXPERT_DFF, d), lambda i, e: (layer, e, 0, 0)),
            pl.BlockSpec((1, d), lambda i, e: (0, 0)),
        ],
        out_specs=pl.BlockSpec((nbt, tb, d), tok),
        out_shape=jax.ShapeDtypeStruct((bsz, t, d), F32),
        scratch_shapes=[pltpu.VMEM((nbt * tb, d), F32)],
        compiler_params=_cparams(("parallel", "arbitrary")),
        name="moe",
    )(h2, gates, x1, mod, wg, wu, wd, gfinal)


def _moe_plan(info, tm):
    n = info.shape[0]
    n_tiles = (2 * n) // tm + N_EXPERTS
    e_flat = info[:, N_EXPERTS:N_EXPERTS + 2].astype(jnp.int32).reshape(2 * n)
    onehot = (e_flat[:, None] == jnp.arange(N_EXPERTS, dtype=jnp.int32)[None, :]).astype(jnp.int32)
    cum = jnp.cumsum(onehot, axis=0)
    rank = jnp.sum(onehot * cum, axis=1) - 1
    counts = cum[-1]
    padded = ((counts + tm - 1) // tm) * tm
    ends = jnp.cumsum(padded)
    slot = jnp.sum(onehot * (ends - padded)[None, :], axis=1) + rank
    row_token = jnp.zeros((n_tiles * tm,), jnp.int32).at[slot].set(jnp.arange(2 * n, dtype=jnp.int32) // 2)
    starts = jnp.arange(n_tiles, dtype=jnp.int32) * tm
    tile_expert = jnp.minimum(jnp.sum((starts[:, None] >= ends[None, :]).astype(jnp.int32), axis=1), N_EXPERTS - 1)
    return (slot.reshape(n, 2), row_token.reshape(n_tiles, 1, tm), tile_expert.astype(jnp.int32),
            (ends[-1:] // tm).astype(jnp.int32))


def _row_gather(idx_ref, count, src_hbm, dst, sem, dst_row):
    def body(r, carry):
        pltpu.make_async_copy(src_hbm.at[pl.ds(idx_ref[0, r], 1)], dst.at[pl.ds(dst_row(r), 1)], sem).start()
        return carry
    lax.fori_loop(0, count, body, 0, unroll=8)


def _row_wait(count, src_hbm, dst, sem):
    def body(r, carry):
        pltpu.make_async_copy(src_hbm.at[pl.ds(0, 1)], dst.at[pl.ds(0, 1)], sem).wait()
        return carry
    lax.fori_loop(0, count, body, 0, unroll=8)


def _moe_gemm_kernel(te_ref, nv_ref, rt_cur, rt_nxt, h_hbm, wg_ref, wu_ref, wd_ref, y_ref, xbuf, sems):
    i = pl.program_id(0)
    nv = nv_ref[0]
    tm = xbuf.shape[1]
    cur = lax.rem(i, 2)

    @pl.when((i == 0) & (nv > 0))
    def _():
        _row_gather(rt_cur, tm, h_hbm, xbuf.at[0], sems.at[0], lambda r: r)

    @pl.when(i + 1 < nv)
    def _():
        _row_gather(rt_nxt, tm, h_hbm, xbuf.at[1 - cur], sems.at[1 - cur], lambda r: r)

    @pl.when(i < nv)
    def _():
        _row_wait(tm, h_hbm, xbuf.at[cur], sems.at[cur])
        x = xbuf[cur].astype(BF16)
        a = _dot(x, wg_ref[...].astype(BF16))
        u = _dot(x, wu_ref[...].astype(BF16))
        y_ref[...] = _dot(((a * _sigmoid(a)) * u).astype(BF16), wd_ref[...].astype(BF16))

    @pl.when(i >= nv)
    def _():
        y_ref[...] = jnp.zeros_like(y_ref)


def _moe_gemm(h2, row_token, tile_expert, n_valid, wg, wu, wd, layer):
    n, d = h2.shape
    n_tiles, _, tm = row_token.shape
    grid_spec = pltpu.PrefetchScalarGridSpec(
        num_scalar_prefetch=2,
        grid=(n_tiles,),
        in_specs=[
            pl.BlockSpec((None, 1, tm), lambda i, te, nv: (i, 0, 0), memory_space=pltpu.SMEM),
            pl.BlockSpec((None, 1, tm), lambda i, te, nv: (jnp.minimum(i + 1, n_tiles - 1), 0, 0),
                         memory_space=pltpu.SMEM),
            pl.BlockSpec(memory_space=pl.ANY),
            pl.BlockSpec((None, None, d, EXPERT_DFF), lambda i, te, nv: (layer, te[i], 0, 0)),
            pl.BlockSpec((None, None, d, EXPERT_DFF), lambda i, te, nv: (layer, te[i], 0, 0)),
            pl.BlockSpec((None, None, EXPERT_DFF, d), lambda i, te, nv: (layer, te[i], 0, 0)),
        ],
        out_specs=pl.BlockSpec((tm, d), lambda i, te, nv: (i, 0)),
        scratch_shapes=[pltpu.VMEM((2, tm, d), F32), pltpu.SemaphoreType.DMA((2,))],
    )
    return pl.pallas_call(
        _moe_gemm_kernel,
        grid_spec=grid_spec,
        out_shape=jax.ShapeDtypeStruct((n_tiles * tm, d), F32),
        compiler_params=_cparams(("arbitrary",)),
        name="moe_gemm",
    )(tile_expert, n_valid, row_token, row_token, h2, wg, wu, wd)


def _moe_combine_kernel(sl_cur, sl_nxt, y_hbm, info_ref, x1_ref, mod_ref, gf_ref, o_ref, ybuf, sems, *, final):
    i = pl.program_id(0)
    n = pl.num_programs(0)
    tc = info_ref.shape[0]
    d = D_MODEL
    cur = lax.rem(i, 2)
    dst_row = lambda q: (q & 1) * tc + (q >> 1)

    @pl.when(i == 0)
    def _():
        _row_gather(sl_cur, 2 * tc, y_hbm, ybuf.at[0], sems.at[0], dst_row)

    @pl.when(i + 1 < n)
    def _():
        _row_gather(sl_nxt, 2 * tc, y_hbm, ybuf.at[1 - cur], sems.at[1 - cur], dst_row)

    _row_wait(2 * tc, y_hbm, ybuf.at[cur], sems.at[cur])
    w_lo = info_ref[:, N_EXPERTS + 2:N_EXPERTS + 3]
    w_hi = info_ref[:, N_EXPERTS + 3:N_EXPERTS + 4]
    moe = w_lo * ybuf[cur, 0:tc, :] + w_hi * ybuf[cur, tc:2 * tc, :]
    x2 = x1_ref[...] + mod_ref[:, 5 * d:6 * d] * moe
    if final:
        ms = jnp.mean(x2 * x2, axis=-1, keepdims=True)
        x2 = x2 * lax.rsqrt(ms + EPS) * gf_ref[...]
    o_ref[...] = x2


def _moe_combine(y, slot, info, x1, mod, gfinal, tc, final):
    bsz, t, d = x1.shape
    n = bsz * t
    nt = n // tc
    per_b = t // tc
    slots = slot.reshape(nt, 1, 2 * tc)
    row = lambda i: (i, 0)
    out = pl.pallas_call(
        functools.partial(_moe_combine_kernel, final=final),
        grid=(nt,),
        in_specs=[
            pl.BlockSpec((None, 1, 2 * tc), lambda i: (i, 0, 0), memory_space=pltpu.SMEM),
            pl.BlockSpec((None, 1, 2 * tc), lambda i: (jnp.minimum(i + 1, nt - 1), 0, 0), memory_space=pltpu.SMEM),
            pl.BlockSpec(memory_space=pl.ANY),
            pl.BlockSpec((tc, LANE), row),
            pl.BlockSpec((tc, d), row),
            pl.BlockSpec((None, 1, 6 * d), lambda i: (i // per_b, 0, 0)),
            pl.BlockSpec((1, d), lambda i: (0, 0)),
        ],
        out_specs=pl.BlockSpec((tc, d), row),
        out_shape=jax.ShapeDtypeStruct((n, d), F32),
        scratch_shapes=[pltpu.VMEM((2, 2 * tc, d), F32), pltpu.SemaphoreType.DMA((2,))],
        compiler_params=_cparams(("arbitrary",)),
        name="moe_combine",
    )(slots, slots, y, info.reshape(n, LANE), x1.reshape(n, d), mod, gfinal)
    return out.reshape(bsz, t, d)


def _rope_tables(pos):
    inv = ROPE_THETA ** (-jnp.arange(0, A_DH, 2, dtype=F32) / A_DH)
    ang = pos.astype(F32)[:, None] * inv[None, :]
    cos, sin = jnp.cos(ang), jnp.sin(ang)
    seg_cos = jnp.concatenate([cos, cos], axis=1)
    seg_sin = jnp.concatenate([-sin, sin], axis=1)
    reps = A_QK // A_DH
    return jnp.tile(seg_cos, (1, reps)), jnp.tile(seg_sin, (1, reps))


def _trunk(x, mods, pos, p, wts, past, tiles):
    bsz, t, d = x.shape
    depth = mods.shape[0]
    cos, sin = _rope_tables(pos)
    lb_p = jax.nn.softmax(p['hgrn_lb'].astype(F32), axis=0)
    lbs = jnp.cumsum(lb_p, axis=0) - lb_p[0:1]
    w_hi, w_mid, w_lo = _split3(p['router_w'].T.astype(F32))
    rw_t = jnp.concatenate([w_hi, w_hi, w_mid, w_hi, w_mid, w_lo], axis=1)
    rb = p['router_b'].reshape(N_EXPERTS, 1).astype(F32)
    gfinal = p['final_norm_g'].reshape(1, d)
    hs, rs, ims = [], [], []
    k_all = jnp.zeros((depth, bsz, t, A_HEADS, LANE), F32)
    v_all = jnp.zeros((depth, bsz, t, A_HEADS, LANE), F32)
    for l in range(depth):
        mod = mods[l]
        q, k, v, k_all, v_all, braw, cu_tm, graw = _in_projection(
            x, mod, p['norm1_g'][l].reshape(1, d), wts['w_in'][l], cos, sin, tiles['tm'], l, depth, k_all, v_all)

        lam_init = 0.8 - 0.6 * math.exp(-0.3 * l)
        lp = p['diff_lambda'][l].astype(F32)
        lam = jnp.exp(jnp.sum(lp[0] * lp[1])) - jnp.exp(jnp.sum(lp[2] * lp[3])) + lam_init
        lam_row = jnp.full((1, LANE), lam, F32)
        subg = p['diff_subln_g'][l].reshape(1, A_DV)
        if past is None:
            ya = _attn_prompt(lam_row, q, k, v, subg, 1.0 - lam_init, tiles['tq'])
        else:
            pk = past[0].reshape(depth, bsz, -1, A_HEADS * 2 * A_DH)
            pv = past[1].reshape(depth, bsz, -1, A_HEADS * A_DV)
            ya = _attn_sample(lam_row, q, pk, pv, l, k, v, subg, 1.0 - lam_init)

        yb, s_h = _hgrn(braw, lbs[l].reshape(1, B_HEADS * B_DK), p['hgrn_norm_g'][l].reshape(1, LANE),
                        None if past is None else past[2], l, tiles['tb'])

        bm, cm, a_tab = _s5_tables(p['s5_a_re'][l], p['s5_a_im'][l], p['s5_b_re'][l], p['s5_b_im'][l],
                                   p['s5_c_re'][l], p['s5_c_im'][l], p['s5_log_dt'][l], bsz)
        if past is None:
            h0 = jnp.zeros((bsz, 2 * C_LANES), F32)
        else:
            h0 = jnp.concatenate([past[3][l].reshape(bsz, C_LANES), past[4][l].reshape(bsz, C_LANES)], axis=1)
        ys_tm, s5_state = _s5(cu_tm.reshape(t * bsz, C_WIDTH), bm, cm, a_tab,
                              p['s5_d'][l].reshape(1, C_WIDTH), h0, tiles['tc'])

        x1, h2, gates = _merge(x, ya, yb, ys_tm.reshape(t, bsz * C_WIDTH), graw, mod,
                               p['norm2_g'][l].reshape(1, d), wts['w_glu'][l], wts['w_a'][l], wts['w_b'][l],
                               wts['w_c'][l], wts['w_out'][l], rw_t, rb, tiles['tm'],
                               F32 if tiles['moe_routed'] else BF16)
        if tiles['moe_routed']:
            slot, row_token, tile_expert, n_valid = _moe_plan(gates.reshape(bsz * t, LANE), tiles['moe_tm'])
            y_rows = _moe_gemm(h2.reshape(bsz * t, d), row_token, tile_expert, n_valid,
                               p['moe_w_gate'], p['moe_w_up'], p['moe_w_down'], l)
            x = _moe_combine(y_rows, slot, gates, x1, mod, gfinal, tiles['moe_tc'], final=(l == depth - 1))
        else:
            x = _moe(h2, gates, x1, mod, p['moe_w_gate'], p['moe_w_up'], p['moe_w_down'], gfinal, l,
                     tiles['moe_nb'], tiles['moe_tb'], final=(l == depth - 1))

        hs.append(s_h)
        rs.append(s5_state[:, :C_LANES].reshape(bsz, C_NGROUPS, C_STATE))
        ims.append(s5_state[:, C_LANES:].reshape(bsz, C_NGROUPS, C_STATE))
    return (x, k_all, v_all, jnp.stack(hs), jnp.stack(rs), jnp.stack(ims))


def kernel(x_prompt, x_sample, cache_k, cache_v, state_hgrn, state_s5_re, state_s5_im, c_prompt, c_sample, w_mod, b_mod, norm1_g, norm2_g, w_in, diff_lambda, diff_subln_g, hgrn_lb, hgrn_norm_g, s5_a_re, s5_a_im, s5_b_re, s5_b_im, s5_c_re, s5_c_im, s5_d, s5_log_dt, s5_w_glu, w_branch_a, w_branch_b, w_branch_c, w_out, router_w, router_b, moe_w_gate, moe_w_up, moe_w_down, final_norm_g):
    p = {
        'norm1_g': norm1_g, 'norm2_g': norm2_g, 'diff_lambda': diff_lambda, 'diff_subln_g': diff_subln_g,
        'hgrn_lb': hgrn_lb, 'hgrn_norm_g': hgrn_norm_g, 's5_a_re': s5_a_re, 's5_a_im': s5_a_im,
        's5_b_re': s5_b_re, 's5_b_im': s5_b_im, 's5_c_re': s5_c_re, 's5_c_im': s5_c_im, 's5_d': s5_d,
        's5_log_dt': s5_log_dt, 'router_w': router_w, 'router_b': router_b,
        'moe_w_gate': moe_w_gate, 'moe_w_up': moe_w_up, 'moe_w_down': moe_w_down, 'final_norm_g': final_norm_g,
    }
    wts = {'w_in': w_in.astype(BF16), 'w_glu': s5_w_glu.astype(BF16), 'w_a': w_branch_a.astype(BF16),
           'w_b': w_branch_b.astype(BF16), 'w_c': w_branch_c.astype(BF16), 'w_out': w_out.astype(BF16)}
    bp, tp, d = x_prompt.shape
    bs, ts, _ = x_sample.shape
    depth = w_mod.shape[0]
    mods = _modulation(jnp.concatenate([c_prompt, c_sample], axis=0), w_mod, b_mod)
    mods = mods.reshape(depth, bp + bs, 1, 6 * d)
    pos_p = jnp.arange(tp)
    pos_s = cache_k.shape[2] + jnp.arange(ts)
    tiles_p = dict(tm=min(256, tp), tq=min(256, tp), tb=min(512, tp), tc=min(64, tp),
                   moe_routed=True, moe_tm=256, moe_tc=min(256, tp))
    tiles_s = dict(tm=ts, tq=ts, tb=ts, tc=ts, moe_routed=False, moe_nb=bs, moe_tb=ts)
    y_p, k_p, v_p, h_p, re_p, im_p = _trunk(x_prompt, mods[:, :bp], pos_p, p, wts, None, tiles_p)
    y_s, k_s, v_s, h_s, re_s, im_s = _trunk(x_sample, mods[:, bp:], pos_s, p, wts,
                                            (cache_k, cache_v, state_hgrn, state_s5_re, state_s5_im), tiles_s)
    return (y_p, y_s, k_p, v_p, h_p, re_p, im_p, k_s, v_s, h_s, re_s, im_s)
```

```python
import functools
import math

import numpy as np
import jax
import jax.numpy as jnp
from jax import lax
from jax.experimental import pallas as pl
from jax.experimental.pallas import tpu as pltpu

F32 = jnp.float32
BF16 = jnp.bfloat16

D_MODEL = 1024
CHUNK = 64
EPS = 1e-6
MASK_NEG = -1e30
ROPE_THETA = 10000.0
A_HEADS = 4
A_DH = 64
A_DV = 128
A_QK = 512
B_HEADS = 4
B_DK = 128
C_WIDTH = 512
C_GROUP = 16
C_NGROUPS = 32
C_STATE = 64
C_LANES = C_NGROUPS * C_STATE
N_EXPERTS = 16
N_EXPERT_GROUPS = 4
EXPERTS_PER_GROUP = 4
EXPERT_DFF = 512
IN_COLS = 7168
COL_AK, COL_AV, COL_B, COL_CU, COL_G = 512, 1024, 1536, 3584, 4096

LANE = 128
LOG2E = math.log2(math.e)
ONES_ROWS = 16
VMEM_LIMIT = 56 * 1024 * 1024


def _cparams(sem):
    return pltpu.CompilerParams(dimension_semantics=sem, vmem_limit_bytes=VMEM_LIMIT)


def _sigmoid(x):
    return 0.5 * jnp.tanh(0.5 * x) + 0.5


def _sigmoid_rel(x):
    return jax.nn.sigmoid(x)


def _dot(a, b):
    return jnp.dot(a, b, preferred_element_type=F32)


def _dot_nt(a, b):
    return lax.dot_general(a, b, (((1,), (1,)), ((), ())), preferred_element_type=F32)


def _dot_tn(a, b):
    return lax.dot_general(a, b, (((0,), (0,)), ((), ())), preferred_element_type=F32)


def _mod_kernel(c_ref, w_ref, b_ref, o_ref):
    c = c_ref[...]
    a = (c * _sigmoid(c)).astype(BF16)
    o_ref[...] = _dot(a, w_ref[...].astype(BF16)) + b_ref[...]


def _modulation(c_all, w_mod, b_mod):
    depth, d, n6 = w_mod.shape
    nb = c_all.shape[0]
    tn = 1536
    return pl.pallas_call(
        _mod_kernel,
        grid=(depth, n6 // tn),
        in_specs=[
            pl.BlockSpec((nb, d), lambda l, j: (0, 0)),
            pl.BlockSpec((None, d, tn), lambda l, j: (l, 0, j)),
            pl.BlockSpec((None, 1, tn), lambda l, j: (l, 0, j)),
        ],
        out_specs=pl.BlockSpec((None, nb, tn), lambda l, j: (l, 0, j)),
        out_shape=jax.ShapeDtypeStruct((depth, nb, n6), F32),
        compiler_params=_cparams(("parallel", "parallel")),
        name="modulation",
    )(c_all, w_mod, b_mod.reshape(depth, 1, n6))


def _swap_half(a):
    parts = []
    for j in range(a.shape[1] // LANE):
        s = a[:, LANE * j:LANE * (j + 1)]
        lane = lax.broadcasted_iota(jnp.int32, s.shape, 1)
        first = (lane & (A_DH - 1)) < (A_DH // 2)
        parts.append(jnp.where(first, pltpu.roll(s, LANE - A_DH // 2, 1), pltpu.roll(s, A_DH // 2, 1)))
    return jnp.concatenate(parts, axis=1)


def _in_kernel(x_ref, mod_ref, g_ref, w_ref, cos_ref, sin_ref, k_all_ref, v_all_ref,
               q_ref, kb_ref, vb_ref, k4_ref, v4_ref, b_ref, cu_ref, gt_ref):
    del k_all_ref, v_all_ref
    x = x_ref[...]
    ms = jnp.mean(x * x, axis=-1, keepdims=True)
    y = x * lax.rsqrt(ms + EPS) * g_ref[...]
    h = y * (1.0 + mod_ref[:, D_MODEL:2 * D_MODEL]) + mod_ref[:, 0:D_MODEL]
    hb = h.astype(BF16)

    def proj(c0, width):
        return _dot(hb, w_ref[:, c0:c0 + width])

    cos = cos_ref[...]
    sin = sin_ref[...]

    def rope(a):
        return a * cos + _swap_half(a) * sin

    q_ref[...] = (rope(proj(0, A_QK)) * (A_DH ** -0.5 * LOG2E)).astype(BF16)
    k = rope(proj(COL_AK, A_QK))
    v = proj(COL_AV, 512)
    kb_ref[...] = k.astype(BF16)
    vb_ref[...] = v.astype(BF16)
    tm = k.shape[0]
    for h in range(A_HEADS):
        k4_ref[pl.ds(h, tm, stride=A_HEADS), :] = k[:, h * LANE:(h + 1) * LANE]
        v4_ref[pl.ds(h, tm, stride=A_HEADS), :] = v[:, h * LANE:(h + 1) * LANE]
    for j in range(4):
        b_ref[:, 512 * j:512 * (j + 1)] = proj(COL_B + 512 * j, 512)
    cu_ref[...] = proj(COL_CU, C_WIDTH)
    for j in range(6):
        gt_ref[:, 512 * j:512 * (j + 1)] = proj(COL_G + 512 * j, 512)


def _in_projection(x, mod, g1, w_in, cos, sin, tm, layer, depth, k_all, v_all):
    bsz, t, d = x.shape
    nt = t // tm
    row = lambda b, i: (b, i, 0)
    cache_spec = pl.BlockSpec((None, None, tm * A_HEADS, LANE), lambda b, i: (layer, b, i, 0))
    cache_shape = jax.ShapeDtypeStruct((depth, bsz, t * A_HEADS, LANE), F32)
    in_specs = [
        pl.BlockSpec((None, tm, d), row),
        pl.BlockSpec((None, 1, 6 * d), lambda b, i: (b, 0, 0)),
        pl.BlockSpec((1, d), lambda b, i: (0, 0)),
        pl.BlockSpec((d, IN_COLS), lambda b, i: (0, 0), pipeline_mode=pl.Buffered(1)),
        pl.BlockSpec((tm, A_QK), lambda b, i: (i, 0)),
        pl.BlockSpec((tm, A_QK), lambda b, i: (i, 0)),
        pl.BlockSpec(memory_space=pl.ANY),
        pl.BlockSpec(memory_space=pl.ANY),
    ]
    args = [x, mod, g1, w_in, cos, sin, k_all, v_all]
    return pl.pallas_call(
        _in_kernel,
        grid=(bsz, nt),
        in_specs=in_specs,
        out_specs=[
            pl.BlockSpec((None, tm, A_QK), row),
            pl.BlockSpec((None, tm, A_QK), row),
            pl.BlockSpec((None, tm, 512), row),
            cache_spec,
            cache_spec,
            pl.BlockSpec((None, tm, 2048), row),
            pl.BlockSpec((tm, C_WIDTH), lambda b, i: (i, b)),
            pl.BlockSpec((None, tm, 3 * d), row),
        ],
        out_shape=[
            jax.ShapeDtypeStruct((bsz, t, A_QK), BF16),
            jax.ShapeDtypeStruct((bsz, t, A_QK), BF16),
            jax.ShapeDtypeStruct((bsz, t, 512), BF16),
            cache_shape,
            cache_shape,
            jax.ShapeDtypeStruct((bsz, t, 2048), F32),
            jax.ShapeDtypeStruct((t, bsz * C_WIDTH), F32),
            jax.ShapeDtypeStruct((bsz, t, 3 * d), F32),
        ],
        input_output_aliases={6: 3, 7: 4},
        compiler_params=_cparams(("parallel", "parallel")),
        name="in_projection",
    )(*args)


def _split_maps(q):
    lane = lax.broadcasted_iota(jnp.int32, q.shape, 1)
    zero = jnp.zeros_like(q)
    return jnp.concatenate([jnp.where(lane < A_DH, q, zero), jnp.where(lane >= A_DH, q, zero)], axis=0)


def _attn_finish(acc, l, lam, g, scale_out, tq):
    o = acc[:tq] / l[:tq] - lam * (acc[tq:] / l[tq:])
    ms = jnp.mean(o * o, axis=-1, keepdims=True)
    return (o * lax.rsqrt(ms + EPS) * g) * scale_out


def _attn_prompt_kernel(lam_ref, q_ref, k_ref, v_ref, g_ref, o_ref, vt, *, tq, scale_out):
    i = pl.program_id(1)
    nkv = vt.shape[0] // A_HEADS

    @pl.when(i == 0)
    def _():
        for h in range(A_HEADS):
            for jj in range(nkv):
                blk = (slice(jj * tq, (jj + 1) * tq), slice(h * LANE, (h + 1) * LANE))
                vt[h * nkv + jj] = jnp.concatenate(
                    [v_ref[blk].astype(F32).T.astype(BF16), jnp.ones((ONES_ROWS, tq), BF16)], axis=0)

    qq = [_split_maps(q_ref[:, h * LANE:(h + 1) * LANE]) for h in range(A_HEADS)]

    def scores(j, h):
        ks = k_ref[pl.ds(pl.multiple_of(j * tq, tq), tq), h * LANE:(h + 1) * LANE]
        return _dot_nt(ks, qq[h])

    def tile(j, carry, s, h, masked):
        m, acc = carry
        if masked:
            kr = lax.broadcasted_iota(jnp.int32, s.shape, 0)
            qc = lax.broadcasted_iota(jnp.int32, s.shape, 1)
            s = jnp.where((kr >> 6) <= ((qc & (tq - 1)) >> 6), s, MASK_NEG)
        mn = jnp.maximum(m, jnp.max(s, axis=0, keepdims=True))
        p = jnp.exp2((s - mn).astype(BF16))
        acc = jnp.exp2(m - mn) * acc + _dot(vt[h * nkv + j], p)
        return mn, acc

    def tiles(j, carries, masked):
        ss = [scores(j, h) for h in range(A_HEADS)]
        return tuple(tile(j, carries[h], ss[h], h, masked) for h in range(A_HEADS))

    init = (jnp.full((1, 2 * tq), MASK_NEG, F32), jnp.zeros((A_DV + ONES_ROWS, 2 * tq), F32))
    carries = lax.fori_loop(0, i, lambda j, c: tiles(j, c, False), (init,) * A_HEADS)
    carries = tiles(i, carries, True)
    for h in range(A_HEADS):
        acc = carries[h][1]
        acc = acc[:A_DV] / acc[A_DV:A_DV + 1]
        o = acc[:, :tq] - lam_ref[0:1, 0:1] * acc[:, tq:]
        ms = jnp.mean(o * o, axis=0, keepdims=True)
        y = (o * lax.rsqrt(ms + EPS) * g_ref[...]) * scale_out
        o_ref[:, h * LANE:(h + 1) * LANE] = y.T.astype(BF16)


def _attn_prompt(lam, q, k, v, g, scale_out, tq):
    bsz, t, _ = q.shape
    assert tq % CHUNK == 0 and tq & (tq - 1) == 0
    kern = functools.partial(_attn_prompt_kernel, tq=tq, scale_out=scale_out)
    return pl.pallas_call(
        kern,
        grid=(bsz, t // tq),
        in_specs=[
            pl.BlockSpec((1, LANE), lambda b, i: (0, 0)),
            pl.BlockSpec((None, tq, A_HEADS * LANE), lambda b, i: (b, i, 0)),
            pl.BlockSpec((None, t, A_HEADS * LANE), lambda b, i: (b, 0, 0)),
            pl.BlockSpec((None, t, A_HEADS * LANE), lambda b, i: (b, 0, 0)),
            pl.BlockSpec((A_DV, 1), lambda b, i: (0, 0)),
        ],
        out_specs=pl.BlockSpec((None, tq, A_HEADS * LANE), lambda b, i: (b, i, 0)),
        out_shape=jax.ShapeDtypeStruct((bsz, t, A_HEADS * A_DV), BF16),
        scratch_shapes=[pltpu.VMEM((A_HEADS * (t // tq), A_DV + ONES_ROWS, tq), BF16)],
        compiler_params=_cparams(("parallel", "arbitrary")),
        name="attn_prompt",
    )(lam, q, k, v, g.reshape(A_DV, 1))


def _attn_sample_kernel(lam_ref, q_ref, pk_ref, pv_ref, k_ref, v_ref, g_ref, o_ref, *, tq, scale_out):
    qq = _split_maps(q_ref[...])
    s_p = _dot_nt(qq, pk_ref[...].astype(BF16))
    s_n = _dot_nt(qq, k_ref[...])
    m = jnp.maximum(jnp.max(s_p, axis=-1, keepdims=True), jnp.max(s_n, axis=-1, keepdims=True))
    p_p = jnp.exp2(s_p - m)
    p_n = jnp.exp2(s_n - m)
    l = jnp.sum(p_p, axis=-1, keepdims=True) + jnp.sum(p_n, axis=-1, keepdims=True)
    acc = _dot(p_p.astype(BF16), pv_ref[...].astype(BF16)) + _dot(p_n.astype(BF16), v_ref[...])
    o_ref[...] = _attn_finish(acc, l, lam_ref[...], g_ref[...], scale_out, tq).astype(BF16)


def _attn_sample(lam, q, cache_k, cache_v, layer, k, v, g, scale_out):
    bsz, t, _ = q.shape
    past = cache_k.shape[2]
    kern = functools.partial(_attn_sample_kernel, tq=t, scale_out=scale_out)
    new = lambda b, h: (b, 0, h)
    old = lambda b, h: (layer, b, 0, h)
    return pl.pallas_call(
        kern,
        grid=(bsz, A_HEADS),
        in_specs=[
            pl.BlockSpec((1, LANE), lambda b, h: (0, 0)),
            pl.BlockSpec((None, t, LANE), new),
            pl.BlockSpec((None, None, past, LANE), old),
            pl.BlockSpec((None, None, past, LANE), old),
            pl.BlockSpec((None, t, LANE), new),
            pl.BlockSpec((None, t, LANE), new),
            pl.BlockSpec((1, LANE), lambda b, h: (0, 0)),
        ],
        out_specs=pl.BlockSpec((None, t, LANE), new),
        out_shape=jax.ShapeDtypeStruct((bsz, t, A_HEADS * A_DV), BF16),
        compiler_params=_cparams(("parallel", "parallel")),
        name="attn_sample",
    )(lam, q, cache_k, cache_v, k, v, g)


def _split3(x):
    hi = x.astype(BF16)
    r = x - hi.astype(F32)
    mid = r.astype(BF16)
    lo = (r - mid.astype(F32)).astype(BF16)
    return hi, mid, lo


HGRN_LEVELS = (5, 4, 3)
HGRN_DIAG = 8


def _hgrn_tables():
    L = CHUNK
    r = np.arange(L)[:, None]
    c = np.arange(L)[None, :]
    sgn, up, pair = [], [], []
    for sh in HGRN_LEVELS:
        upper = ((r >> sh) & 1) == 1
        up.append(np.broadcast_to(upper, (L, B_DK)))
        sgn.append(np.broadcast_to(np.where(upper, 1.0, -1.0), (L, B_DK)))
        pair.append((((r >> sh) & 1) == 1) & ((r >> (sh + 1)) == (c >> (sh + 1))) & (((c >> sh) & 1) == 0))
    diag = [(c == r - d) & ((r & (HGRN_DIAG - 1)) >= d) for d in range(HGRN_DIAG)]
    tril = (c <= r).astype(np.float32)
    rowm = jnp.asarray(np.stack(sgn + up).astype(np.float32))
    sqm = jnp.asarray(np.stack(pair + diag).astype(np.float32))
    tril3 = jnp.asarray(np.concatenate([tril, tril, tril], axis=1), dtype=BF16)
    return tril3, rowm, sqm


def _hgrn_chunk(heads, gn, tril3, rowm_ref, sqm_ref):
    L = CHUNK
    nl = len(HGRN_LEVELS)
    n = len(heads)

    fs, kfs, qs, bs = [], [], [], []
    for qraw, fraw, v, graw, lb, st in heads:
        f = lb + (1.0 - lb) * _sigmoid_rel(fraw)
        hi, mid, lo = _split3(jnp.log(f))
        fs.append(f)
        kfs.append(1.0 - f)
        qs.append(qraw * _sigmoid(qraw))
        bs.append(_dot(tril3, jnp.concatenate([hi, mid, lo], axis=0)))

    os_, atts, sts, vbs = [], [], [], []
    for i in range(n):
        qraw, fraw, v, graw, lb, st = heads[i]
        q, kf, b = qs[i], kfs[i], bs[i]
        vb = v.astype(BF16)
        o = _dot_nt((q * jnp.exp(b)).astype(BF16), st.astype(BF16))
        att = jnp.zeros((L, L), F32)
        for li, sh in enumerate(HGRN_LEVELS):
            m = 1 << sh
            ref_rows = [jnp.broadcast_to(b[(2 * j + 1) * m - 1:(2 * j + 1) * m, :], (2 * m, B_DK))
                        for j in range(L // (2 * m))]
            ref = ref_rows[0] if len(ref_rows) == 1 else jnp.concatenate(ref_rows, axis=0)
            w = jnp.exp((b - ref) * rowm_ref[li])
            wq = w * rowm_ref[nl + li]
            att = att + sqm_ref[li] * _dot_nt((q * wq).astype(BF16), (kf * (w - wq)).astype(BF16))
        b_last = b[L - 1:L, :]
        kdec = (kf * jnp.exp(b_last - b)).astype(BF16)
        sts.append(st * jnp.exp(b_last) + _dot_tn(vb, kdec))
        os_.append(o)
        atts.append(att)
        vbs.append(vb)

    outs = []
    for i in range(n):
        q, f, att = qs[i], fs[i], atts[i]
        hd = kfs[i]
        for d in range(HGRN_DIAG):
            if d > 0:
                hd = f * pltpu.roll(hd, 1, 0)
            att = att + sqm_ref[nl + d] * jnp.sum(q * hd, axis=-1, keepdims=True)
        outs.append(os_[i] + _dot(att.astype(BF16), vbs[i]))

    res = []
    for i in range(n):
        o, graw = outs[i], heads[i][3]
        ms = jnp.mean(o * o, axis=-1, keepdims=True)
        res.append((o * lax.rsqrt(ms + EPS) * gn * (graw * _sigmoid(graw)), sts[i]))
    return res


def _hgrn_kernel(*refs, nchunk, has_s0):
    if has_s0:
        q_ref, f_ref, v_ref, g_ref, lb_ref, gn_ref, tril_ref, rowm_ref, sqm_ref, s0_ref, y_ref, sf_ref, st = refs
    else:
        q_ref, f_ref, v_ref, g_ref, lb_ref, gn_ref, tril_ref, rowm_ref, sqm_ref, y_ref, sf_ref, st = refs
    c = pl.program_id(1)

    @pl.when(c == 0)
    def _():
        for h in range(B_HEADS):
            st[h] = s0_ref[h].T if has_s0 else jnp.zeros((LANE, B_DK), F32)

    gn = gn_ref[...]
    tril3 = tril_ref[...]

    def body(n, carry):
        sl = pl.ds(pl.multiple_of(n * CHUNK, CHUNK), CHUNK)
        hsl = [slice(h * LANE, (h + 1) * LANE) for h in range(B_HEADS)]
        heads = [(q_ref[sl, hs], f_ref[sl, hs], v_ref[sl, hs], g_ref[sl, hs], lb_ref[:, hs], st[h])
                 for h, hs in enumerate(hsl)]
        for h, (y, st_new) in enumerate(_hgrn_chunk(heads, gn, tril3, rowm_ref, sqm_ref)):
            st[h] = st_new
            y_ref[sl, hsl[h]] = y.astype(BF16)
        return carry

    lax.fori_loop(0, nchunk, body, 0, unroll=min(2, nchunk))

    @pl.when(c == pl.num_programs(1) - 1)
    def _():
        for h in range(B_HEADS):
            sf_ref[h] = st[h].T


def _hgrn(braw, lb, gn, s0, layer, tb):
    bsz, t, _ = braw.shape
    has_s0 = s0 is not None
    kern = functools.partial(_hgrn_kernel, nchunk=tb // CHUNK, has_s0=has_s0)
    width = B_HEADS * LANE
    tril3, rowm, sqm = _hgrn_tables()

    def col(j):
        return pl.BlockSpec((None, tb, width), lambda b, c: (b, c, j))

    def const(shape):
        return pl.BlockSpec(shape, lambda b, c: (0,) * len(shape))

    in_specs = [col(0), col(1), col(2), col(3), const((1, width)), const((1, LANE)),
                const(tril3.shape), const(rowm.shape), const(sqm.shape)]
    args = [braw, braw, braw, braw, lb, gn, tril3, rowm, sqm]
    if has_s0:
        in_specs.append(pl.BlockSpec((None, None, B_HEADS, B_DK, LANE), lambda b, c: (layer, b, 0, 0, 0)))
        args.append(s0)
    return pl.pallas_call(
        kern,
        grid=(bsz, t // tb),
        in_specs=in_specs,
        out_specs=[
            pl.BlockSpec((None, tb, width), lambda b, c: (b, c, 0)),
            pl.BlockSpec((None, B_HEADS, B_DK, LANE), lambda b, c: (b, 0, 0, 0)),
        ],
        out_shape=[
            jax.ShapeDtypeStruct((bsz, t, width), BF16),
            jax.ShapeDtypeStruct((bsz, B_HEADS, B_DK, LANE), F32),
        ],
        scratch_shapes=[pltpu.VMEM((B_HEADS, LANE, B_DK), F32)],
        compiler_params=_cparams(("parallel", "arbitrary")),
        name="hgrn",
    )(*args)


S5_COLS = 512


def _s5_kernel(u_ref, bm_ref, cm_ref, a_ref, d_ref, h0_ref, y_ref, hf_ref, xs, hs, *, tc, nb):
    step_id = pl.program_id(0)

    @pl.when(step_id == 0)
    def _():
        hs[...] = h0_ref[...]

    u = u_ref[...]
    xs[...] = _dot(u.astype(BF16), bm_ref[...])
    for cc in range(C_LANES // S5_COLS):
        re = slice(cc * S5_COLS, (cc + 1) * S5_COLS)
        im = slice(C_LANES + cc * S5_COLS, C_LANES + (cc + 1) * S5_COLS)
        ar = a_ref[0:nb, re]
        ai = a_ref[nb:2 * nb, re]

        def step(t, carry):
            hr, hi = carry
            rows = pl.ds(pl.multiple_of(t * nb, nb), nb)
            nr = ar * hr - ai * hi + xs[rows, re]
            ni = ar * hi + ai * hr + xs[rows, im]
            xs[rows, re] = nr
            xs[rows, im] = ni
            return nr, ni

        hr, hi = lax.fori_loop(0, tc, step, (hs[:, re], hs[:, im]), unroll=8)
        hs[:, re] = hr
        hs[:, im] = hi
    y = _dot(xs[...].astype(BF16), cm_ref[...]) + d_ref[...] * u
    y_ref[...] = y.astype(BF16)
    hf_ref[...] = hs[...]


def _s5(u_tm, bm, cm, a, d, h0, tc):
    n, _ = u_tm.shape
    nb = h0.shape[0]
    rows = tc * nb
    kern = functools.partial(_s5_kernel, tc=tc, nb=nb)
    const = lambda i: (0, 0)
    return pl.pallas_call(
        kern,
        grid=(n // rows,),
        in_specs=[
            pl.BlockSpec((rows, C_WIDTH), lambda i: (i, 0)),
            pl.BlockSpec((C_WIDTH, 2 * C_LANES), const, pipeline_mode=pl.Buffered(1)),
            pl.BlockSpec((2 * C_LANES, C_WIDTH), const, pipeline_mode=pl.Buffered(1)),
            pl.BlockSpec((2 * nb, C_LANES), const),
            pl.BlockSpec((1, C_WIDTH), const),
            pl.BlockSpec((nb, 2 * C_LANES), const),
        ],
        out_specs=[
            pl.BlockSpec((rows, C_WIDTH), lambda i: (i, 0)),
            pl.BlockSpec((nb, 2 * C_LANES), const),
        ],
        out_shape=[
            jax.ShapeDtypeStruct((n, C_WIDTH), BF16),
            jax.ShapeDtypeStruct((nb, 2 * C_LANES), F32),
        ],
        scratch_shapes=[pltpu.VMEM((rows, 2 * C_LANES), F32), pltpu.VMEM((nb, 2 * C_LANES), F32)],
        compiler_params=_cparams(("arbitrary",)),
        name="s5",
    )(u_tm, bm, cm, a, d, h0)


def _s5_tables(a_re, a_im, b_re, b_im, c_re, c_im, log_dt, nb):
    dt = jnp.exp(log_dt)[:, None]
    mag = jnp.exp(a_re * dt)
    abr, abi = mag * jnp.cos(a_im * dt), mag * jnp.sin(a_im * dt)
    den = a_re * a_re + a_im * a_im
    coef_r = ((abr - 1.0) * a_re + abi * a_im) / den
    coef_i = (abi * a_re - (abr - 1.0) * a_im) / den
    fr = coef_r[..., None] * b_re - coef_i[..., None] * b_im
    fi = coef_r[..., None] * b_im + coef_i[..., None] * b_re
    eye = jnp.eye(C_NGROUPS, dtype=F32)

    def blockdiag_in(m):
        return jnp.einsum('gpj,gh->gjhp', m, eye).reshape(C_WIDTH, C_LANES)

    def blockdiag_out(m):
        return jnp.einsum('gjp,gh->gphj', m, eye).reshape(C_LANES, C_WIDTH)

    bm = jnp.concatenate([blockdiag_in(fr), blockdiag_in(fi)], axis=1).astype(BF16)
    cm = jnp.concatenate([blockdiag_out(c_re), -blockdiag_out(c_im)], axis=0).astype(BF16)
    a = jnp.concatenate([jnp.broadcast_to(abr.reshape(1, C_LANES), (nb, C_LANES)),
                         jnp.broadcast_to(abi.reshape(1, C_LANES), (nb, C_LANES))], axis=0)
    return bm, cm, a


def _pair_max_sum(v):
    best = v[0:1] + v[1:2]
    for i, j in ((0, 2), (0, 3), (1, 2), (1, 3), (2, 3)):
        best = jnp.maximum(best, v[i:i + 1] + v[j:j + 1])
    return best


def _route(logits_t, rb):
    mx = jnp.max(logits_t, axis=0, keepdims=True)
    ex = jnp.exp(logits_t - mx)
    s = ex / jnp.sum(ex, axis=0, keepdims=True)
    sb = s + rb
    g = EXPERTS_PER_GROUP
    best = _pair_max_sum(sb[0:g])
    sel = jnp.zeros(best.shape, jnp.int32)
    for gi in range(1, N_EXPERT_GROUPS):
        sc = _pair_max_sum(sb[gi * g:(gi + 1) * g])
        take = sc > best
        best = jnp.where(take, sc, best)
        sel = jnp.where(take, gi, sel)
    sb4 = jnp.zeros((g,) + best.shape[1:], F32)
    s4 = jnp.zeros((g,) + best.shape[1:], F32)
    for gi in range(N_EXPERT_GROUPS):
        on = sel == gi
        sb4 = jnp.where(on, sb[gi * g:(gi + 1) * g], sb4)
        s4 = jnp.where(on, s[gi * g:(gi + 1) * g], s4)
    rows, keeps = [], []
    for e in range(g):
        rank = jnp.zeros(best.shape, jnp.int32)
        for j in range(g):
            if j == e:
                continue
            ahead = (sb4[j:j + 1] > sb4[e:e + 1]) if j > e else (sb4[j:j + 1] >= sb4[e:e + 1])
            rank = rank + ahead.astype(jnp.int32)
        rows.append(jnp.where(rank < 2, s4[e:e + 1], 0.0))
        keeps.append(jnp.where(rank < 2, 1.0, 0.0))
    w4 = jnp.concatenate(rows, axis=0)
    w4 = w4 / jnp.sum(w4, axis=0, keepdims=True)
    keep4 = jnp.concatenate(keeps, axis=0)
    gates = jnp.concatenate([jnp.where(sel == gi, w4, 0.0) for gi in range(N_EXPERT_GROUPS)], axis=0)
    chosen = jnp.concatenate([jnp.where(sel == gi, keep4, 0.0) for gi in range(N_EXPERT_GROUPS)], axis=0) > 0.5
    eid = lax.broadcasted_iota(jnp.int32, gates.shape, 0).astype(F32)
    e_lo = jnp.min(jnp.where(chosen, eid, float(N_EXPERTS)), axis=0, keepdims=True)
    e_hi = jnp.max(jnp.where(chosen, eid, -1.0), axis=0, keepdims=True)
    w_lo = jnp.sum(jnp.where(eid == e_lo, gates, 0.0), axis=0, keepdims=True)
    w_hi = jnp.sum(jnp.where(eid == e_hi, gates, 0.0), axis=0, keepdims=True)
    n = gates.shape[1]
    tri = jnp.where(lax.broadcasted_iota(jnp.int32, (n, n), 0) <= lax.broadcasted_iota(jnp.int32, (n, n), 1),
                    1.0, 0.0).astype(BF16)
    chosen_f = jnp.where(chosen, 1.0, 0.0)
    cum = _dot(chosen_f.astype(BF16), tri)
    before = cum - chosen_f
    r_lo = jnp.sum(jnp.where(eid == e_lo, before, 0.0), axis=0, keepdims=True)
    r_hi = jnp.sum(jnp.where(eid == e_hi, before, 0.0), axis=0, keepdims=True)
    return gates, jnp.concatenate([e_lo, e_hi, w_lo, w_hi, r_lo, r_hi], axis=0), cum[:, n - 1:n]


def _merge_kernel(x_ref, ya_ref, yb_ref, ys_ref, gt_ref, mod_ref, g2_ref,
                  wglu_ref, wa_ref, wb_ref, wc_ref, wout_ref, rw_ref, rb_ref,
                  x1_ref, h2_ref, gates_ref, counts_ref):
    d = D_MODEL
    z = _dot(ys_ref[...], wglu_ref[...])
    yc = (z[:, :C_WIDTH] * _sigmoid(z[:, C_WIDTH:])).astype(BF16)
    merged = (_sigmoid(gt_ref[:, 0:d]) * _dot(ya_ref[...], wa_ref[...])
              + _sigmoid(gt_ref[:, d:2 * d]) * _dot(yb_ref[...], wb_ref[...])
              + _sigmoid(gt_ref[:, 2 * d:3 * d]) * _dot(yc, wc_ref[...]))
    mix = _dot(merged.astype(BF16), wout_ref[...])
    x1 = x_ref[...] + mod_ref[:, 2 * d:3 * d] * mix
    x1_ref[...] = x1
    ms = jnp.mean(x1 * x1, axis=-1, keepdims=True)
    h2 = (x1 * lax.rsqrt(ms + EPS) * g2_ref[...]) * (1.0 + mod_ref[:, 4 * d:5 * d]) + mod_ref[:, 3 * d:4 * d]
    h_hi, h_mid, h_lo = _split3(h2)
    h2_ref[...] = h_hi.astype(h2_ref.dtype)
    hcat = jnp.concatenate([h_hi, h_mid, h_hi, h_lo, h_mid, h_hi], axis=1)
    gates_t, picks_t, counts = _route(_dot_nt(rw_ref[...], hcat), rb_ref[...])
    pad = jnp.zeros((LANE - N_EXPERTS - picks_t.shape[0], gates_t.shape[1]), F32)
    gates_ref[...] = jnp.concatenate([gates_t, picks_t, pad], axis=0).T
    counts_ref[...] = jnp.broadcast_to(counts, counts_ref.shape)


def _merge(x, ya, yb, ys_tm, graw, mod, g2, wglu, wa, wb, wc, wout, rw_t, rb, tm, h2_dtype):
    bsz, t, d = x.shape
    row = lambda b, i: (b, i, 0)
    const = lambda b, i: (0, 0)

    def wspec(shape):
        return pl.BlockSpec(shape, const, pipeline_mode=pl.Buffered(1))

    return pl.pallas_call(
        _merge_kernel,
        grid=(bsz, t // tm),
        in_specs=[
            pl.BlockSpec((None, tm, d), row),
            pl.BlockSpec((None, tm, 512), row),
            pl.BlockSpec((None, tm, 512), row),
            pl.BlockSpec((tm, C_WIDTH), lambda b, i: (i, b)),
            pl.BlockSpec((None, tm, 3 * d), row),
            pl.BlockSpec((None, 1, 6 * d), lambda b, i: (b, 0, 0)),
            pl.BlockSpec((1, d), const),
            wspec((C_WIDTH, 2 * C_WIDTH)), wspec((512, d)), wspec((512, d)), wspec((C_WIDTH, d)),
            wspec((d, d)), wspec((N_EXPERTS, 6 * d)), wspec((N_EXPERTS, 1)),
        ],
        out_specs=[
            pl.BlockSpec((None, tm, d), row),
            pl.BlockSpec((None, tm, d), row),
            pl.BlockSpec((None, tm, LANE), row),
            pl.BlockSpec((None, None, N_EXPERTS, LANE), lambda b, i: (b, i, 0, 0)),
        ],
        out_shape=[
            jax.ShapeDtypeStruct((bsz, t, d), F32),
            jax.ShapeDtypeStruct((bsz, t, d), h2_dtype),
            jax.ShapeDtypeStruct((bsz, t, LANE), F32),
            jax.ShapeDtypeStruct((bsz, t // tm, N_EXPERTS, LANE), F32),
        ],
        compiler_params=_cparams(("parallel", "parallel")),
        name="merge_router",
    )(x, ya, yb, ys_tm, graw, mod, g2, wglu, wa, wb, wc, wout, rw_t, rb)


def _moe_kernel(h_ref, gates_ref, x1_ref, mod_ref, wg_ref, wu_ref, wd_ref, gf_ref, o_ref, acc, *, final):
    e = pl.program_id(1)
    nbt, tb, d = h_ref.shape
    tm = nbt * tb

    @pl.when(e == 0)
    def _():
        acc[...] = jnp.zeros_like(acc)

    h = h_ref[...].reshape(tm, d)
    a = _dot(h, wg_ref[...].astype(BF16))
    u = _dot(h, wu_ref[...].astype(BF16))
    gates = gates_ref[...].reshape(tm, LANE)
    lane = lax.broadcasted_iota(jnp.int32, gates.shape, 1)
    gcol = jnp.sum(jnp.where(lane == e, gates, 0.0), axis=-1, keepdims=True)
    hid = (a * _sigmoid(a)) * u * gcol
    acc[...] += _dot(hid.astype(BF16), wd_ref[...].astype(BF16))

    @pl.when(e == pl.num_programs(1) - 1)
    def _():
        x2 = x1_ref[...] + mod_ref[:, :, 5 * d:6 * d] * acc[...].reshape(nbt, tb, d)
        if final:
            ms = jnp.mean(x2 * x2, axis=-1, keepdims=True)
            x2 = x2 * lax.rsqrt(ms + EPS) * gf_ref[...]
        o_ref[...] = x2


def _moe(h2, gates, x1, mod, wg, wu, wd, gfinal, layer, nbt, tb, final):
    bsz, t, d = x1.shape
    kern = functools.partial(_moe_kernel, final=final)
    tok = lambda i, e: (i // (t // tb), i % (t // tb), 0) if nbt == 1 else (i, 0, 0)
    nsteps = (bsz // nbt) * (t // tb)
    return pl.pallas_call(
        kern,
        grid=(nsteps, N_EXPERTS),
        in_specs=[
            pl.BlockSpec((nbt, tb, d), tok),
            pl.BlockSpec((nbt, tb, LANE), tok),
            pl.BlockSpec((nbt, tb, d), tok),
            pl.BlockSpec((nbt, 1, 6 * d), lambda i, e: ((i // (t // tb)) if nbt == 1 else i, 0, 0)),
            pl.BlockSpec((None, None, d, EXPERT_DFF), lambda i, e: (layer, e, 0, 0)),
            pl.BlockSpec((None, None, d, EXPERT_DFF), lambda i, e: (layer, e, 0, 0)),
            pl.BlockSpec((None, None, EXPERT_DFF, d), lambda i, e: (layer, e, 0, 0)),
            pl.BlockSpec((1, d), lambda i, e: (0, 0)),
        ],
        out_specs=pl.BlockSpec((nbt, tb, d), tok),
        out_shape=jax.ShapeDtypeStruct((bsz, t, d), F32),
        scratch_shapes=[pltpu.VMEM((nbt * tb, d), F32)],
        compiler_params=_cparams(("parallel", "arbitrary")),
        name="moe",
    )(h2, gates, x1, mod, wg, wu, wd, gfinal)


def _moe_plan(info, counts, tok_tile, tm):
    n = info.shape[0]
    n_tiles = (2 * n) // tm + N_EXPERTS
    counts = counts.astype(jnp.int32)
    total = jnp.sum(counts, axis=0)
    padded = ((total + tm - 1) // tm) * tm
    ends = jnp.cumsum(padded)
    offs = ends - padded
    tile_base = offs[None, :] + jnp.cumsum(counts, axis=0) - counts
    base_tok = jnp.repeat(tile_base, tok_tile, axis=0)
    experts = jnp.arange(N_EXPERTS, dtype=jnp.int32)[None, :]
    slots = []
    for k in range(2):
        e = info[:, N_EXPERTS + k].astype(jnp.int32)
        r = info[:, N_EXPERTS + 4 + k].astype(jnp.int32)
        slots.append(jnp.sum(jnp.where(e[:, None] == experts, base_tok, 0), axis=1) + r)
    starts = jnp.arange(n_tiles, dtype=jnp.int32) * tm
    tile_expert = jnp.minimum(jnp.sum((starts[:, None] >= ends[None, :]).astype(jnp.int32), axis=1), N_EXPERTS - 1)
    fill = jnp.stack([offs + total, ends]).astype(jnp.int32)
    return jnp.stack(slots, axis=1), tile_expert.astype(jnp.int32), (ends[-1:] // tm).astype(jnp.int32), fill


def _row_wait(count, src, dst, sem):
    def body(r, carry):
        pltpu.make_async_copy(src.at[pl.ds(0, 1)], dst.at[pl.ds(0, 1)], sem).wait()
        return carry
    lax.fori_loop(0, count, body, 0, unroll=8)


def _moe_dispatch_kernel(fill_ref, sl_ref, h_ref, xs_hbm, zrow, sems, *, n_rows, tm):
    i = pl.program_id(0)
    sem = sems.at[0]
    td = h_ref.shape[0]
    n_pad = N_EXPERTS * tm

    @pl.when(i == 0)
    def _():
        zrow[...] = jnp.zeros_like(zrow)

        def zero_rows(lo, hi):
            def body(s, carry):
                pltpu.make_async_copy(zrow.at[pl.ds(0, 1)], xs_hbm.at[pl.ds(s, 1)], sem).start()
                return carry
            lax.fori_loop(lo, hi, body, 0)

        for e in range(N_EXPERTS):
            zero_rows(fill_ref[0, e], fill_ref[1, e])
        zero_rows(fill_ref[1, N_EXPERTS - 1], n_rows)
        _row_wait(n_pad, zrow, xs_hbm, sem)

    for r in range(td):
        for k in range(2):
            pltpu.make_async_copy(h_ref.at[pl.ds(r, 1)], xs_hbm.at[pl.ds(sl_ref[0, k * td + r], 1)], sem).start()
    _row_wait(2 * td, h_ref, xs_hbm, sem)


def _moe_dispatch(h2, slot, fill, n_rows, td, tm):
    n, d = h2.shape
    nt = n // td
    slots = slot.reshape(nt, td, 2).transpose(0, 2, 1).reshape(nt, 1, 2 * td)
    grid_spec = pltpu.PrefetchScalarGridSpec(
        num_scalar_prefetch=1,
        grid=(nt,),
        in_specs=[
            pl.BlockSpec((None, 1, 2 * td), lambda i, fl: (i, 0, 0), memory_space=pltpu.SMEM),
            pl.BlockSpec((td, d), lambda i, fl: (i, 0)),
        ],
        out_specs=pl.BlockSpec(memory_space=pl.ANY),
        scratch_shapes=[pltpu.VMEM((8, d), F32), pltpu.SemaphoreType.DMA((1,))],
    )
    return pl.pallas_call(
        functools.partial(_moe_dispatch_kernel, n_rows=n_rows, tm=tm),
        grid_spec=grid_spec,
        out_shape=jax.ShapeDtypeStruct((n_rows, d), F32),
        compiler_params=_cparams(("arbitrary",)),
        name="moe_dispatch",
    )(fill, slots, h2), slots


def _moe_gemm_kernel(te_ref, nv_ref, x_ref, wg_ref, wu_ref, wd_ref, y_ref):
    i = pl.program_id(0)
    nv = nv_ref[0]

    @pl.when(i < nv)
    def _():
        x = x_ref[...].astype(BF16)
        a = _dot(x, wg_ref[...].astype(BF16))
        u = _dot(x, wu_ref[...].astype(BF16))
        y_ref[...] = _dot(((a * _sigmoid(a)) * u).astype(BF16), wd_ref[...].astype(BF16))

    @pl.when(i >= nv)
    def _():
        y_ref[...] = jnp.zeros_like(y_ref)


def _moe_gemm(xs, tile_expert, n_valid, wg, wu, wd, layer, tm):
    n_rows, d = xs.shape
    n_tiles = n_rows // tm
    grid_spec = pltpu.PrefetchScalarGridSpec(
        num_scalar_prefetch=2,
        grid=(n_tiles,),
        in_specs=[
            pl.BlockSpec((tm, d), lambda i, te, nv: (i, 0)),
            pl.BlockSpec((None, None, d, EXPERT_DFF), lambda i, te, nv: (layer, te[i], 0, 0)),
            pl.BlockSpec((None, None, d, EXPERT_DFF), lambda i, te, nv: (layer, te[i], 0, 0)),
            pl.BlockSpec((None, None, EXPERT_DFF, d), lambda i, te, nv: (layer, te[i], 0, 0)),
        ],
        out_specs=pl.BlockSpec((tm, d), lambda i, te, nv: (i, 0)),
    )
    return pl.pallas_call(
        _moe_gemm_kernel,
        grid_spec=grid_spec,
        out_shape=jax.ShapeDtypeStruct((n_rows, d), F32),
        compiler_params=_cparams(("arbitrary",)),
        name="moe_gemm",
    )(tile_expert, n_valid, xs, wg, wu, wd)


def _moe_combine_kernel(sl_cur, sl_nxt, y_hbm, info_ref, x1_ref, mod_ref, gf_ref, o_ref, ybuf, sems, *, final):
    i = pl.program_id(0)
    n = pl.num_programs(0)
    tc = info_ref.shape[0]
    d = D_MODEL
    cur = lax.rem(i, 2)

    def gather(sl_ref, buf):
        for q in range(2 * tc):
            pltpu.make_async_copy(y_hbm.at[pl.ds(sl_ref[0, q], 1)], ybuf.at[buf, pl.ds(q, 1)], sems.at[buf]).start()

    @pl.when(i == 0)
    def _():
        gather(sl_cur, 0)

    for buf in range(2):
        @pl.when((i + 1 < n) & (cur == 1 - buf))
        def _():
            gather(sl_nxt, buf)

    _row_wait(2 * tc, y_hbm, ybuf.at[cur], sems.at[cur])
    w_lo = info_ref[:, N_EXPERTS + 2:N_EXPERTS + 3]
    w_hi = info_ref[:, N_EXPERTS + 3:N_EXPERTS + 4]
    moe = w_lo * ybuf[cur, 0:tc, :] + w_hi * ybuf[cur, tc:2 * tc, :]
    x2 = x1_ref[...] + mod_ref[:, 5 * d:6 * d] * moe
    if final:
        ms = jnp.mean(x2 * x2, axis=-1, keepdims=True)
        x2 = x2 * lax.rsqrt(ms + EPS) * gf_ref[...]
    o_ref[...] = x2


def _moe_combine(y, slots, info, x1, mod, gfinal, final):
    bsz, t, d = x1.shape
    n = bsz * t
    nt = slots.shape[0]
    tc = slots.shape[2] // 2
    per_b = t // tc
    row = lambda i: (i, 0)
    out = pl.pallas_call(
        functools.partial(_moe_combine_kernel, final=final),
        grid=(nt,),
        in_specs=[
            pl.BlockSpec((None, 1, 2 * tc), lambda i: (i, 0, 0), memory_space=pltpu.SMEM),
            pl.BlockSpec((None, 1, 2 * tc), lambda i: (jnp.minimum(i + 1, nt - 1), 0, 0), memory_space=pltpu.SMEM),
            pl.BlockSpec(memory_space=pl.ANY),
            pl.BlockSpec((tc, LANE), row),
            pl.BlockSpec((tc, d), row),
            pl.BlockSpec((None, 1, 6 * d), lambda i: (i // per_b, 0, 0)),
            pl.BlockSpec((1, d), lambda i: (0, 0)),
        ],
        out_specs=pl.BlockSpec((tc, d), row),
        out_shape=jax.ShapeDtypeStruct((n, d), F32),
        scratch_shapes=[pltpu.VMEM((2, 2 * tc, d), F32), pltpu.SemaphoreType.DMA((2,))],
        compiler_params=_cparams(("arbitrary",)),
        name="moe_combine",
    )(slots, slots, y, info.reshape(n, LANE), x1.reshape(n, d), mod, gfinal)
    return out.reshape(bsz, t, d)


def _rope_tables(pos):
    inv = ROPE_THETA ** (-jnp.arange(0, A_DH, 2, dtype=F32) / A_DH)
    ang = pos.astype(F32)[:, None] * inv[None, :]
    cos, sin = jnp.cos(ang), jnp.sin(ang)
    seg_cos = jnp.concatenate([cos, cos], axis=1)
    seg_sin = jnp.concatenate([-sin, sin], axis=1)
    reps = A_QK // A_DH
    return jnp.tile(seg_cos, (1, reps)), jnp.tile(seg_sin, (1, reps))


def _trunk(x, mods, pos, p, wts, past, tiles):
    bsz, t, d = x.shape
    depth = mods.shape[0]
    cos, sin = _rope_tables(pos)
    lb_p = jax.nn.softmax(p['hgrn_lb'].astype(F32), axis=0)
    lbs = jnp.cumsum(lb_p, axis=0) - lb_p[0:1]
    w_hi, w_mid, w_lo = _split3(p['router_w'].T.astype(F32))
    rw_t = jnp.concatenate([w_hi, w_hi, w_mid, w_hi, w_mid, w_lo], axis=1)
    rb = p['router_b'].reshape(N_EXPERTS, 1).astype(F32)
    gfinal = p['final_norm_g'].reshape(1, d)
    hs, rs, ims = [], [], []
    k_all = jnp.zeros((depth, bsz, t * A_HEADS, LANE), F32)
    v_all = jnp.zeros((depth, bsz, t * A_HEADS, LANE), F32)
    for l in range(depth):
        mod = mods[l]
        q, k, v, k_all, v_all, braw, cu_tm, graw = _in_projection(
            x, mod, p['norm1_g'][l].reshape(1, d), wts['w_in'][l], cos, sin, tiles['tm'], l, depth, k_all, v_all)

        lam_init = 0.8 - 0.6 * math.exp(-0.3 * l)
        lp = p['diff_lambda'][l].astype(F32)
        lam = jnp.exp(jnp.sum(lp[0] * lp[1])) - jnp.exp(jnp.sum(lp[2] * lp[3])) + lam_init
        lam_row = jnp.full((1, LANE), lam, F32)
        subg = p['diff_subln_g'][l].reshape(1, A_DV)
        if past is None:
            ya = _attn_prompt(lam_row, q, k, v, subg, 1.0 - lam_init, tiles['tq'])
        else:
            pk = past[0].reshape(depth, bsz, -1, A_HEADS * 2 * A_DH)
            pv = past[1].reshape(depth, bsz, -1, A_HEADS * A_DV)
            ya = _attn_sample(lam_row, q, pk, pv, l, k, v, subg, 1.0 - lam_init)

        yb, s_h = _hgrn(braw, lbs[l].reshape(1, B_HEADS * B_DK), p['hgrn_norm_g'][l].reshape(1, LANE),
                        None if past is None else past[2], l, tiles['tb'])

        bm, cm, a_tab = _s5_tables(p['s5_a_re'][l], p['s5_a_im'][l], p['s5_b_re'][l], p['s5_b_im'][l],
                                   p['s5_c_re'][l], p['s5_c_im'][l], p['s5_log_dt'][l], bsz)
        if past is None:
            h0 = jnp.zeros((bsz, 2 * C_LANES), F32)
        else:
            h0 = jnp.concatenate([past[3][l].reshape(bsz, C_LANES), past[4][l].reshape(bsz, C_LANES)], axis=1)
        ys_tm, s5_state = _s5(cu_tm.reshape(t * bsz, C_WIDTH), bm, cm, a_tab,
                              p['s5_d'][l].reshape(1, C_WIDTH), h0, tiles['tc'])

        x1, h2, gates, counts = _merge(x, ya, yb, ys_tm.reshape(t, bsz * C_WIDTH), graw, mod,
                               p['norm2_g'][l].reshape(1, d), wts['w_glu'][l], wts['w_a'][l], wts['w_b'][l],
                               wts['w_c'][l], wts['w_out'][l], rw_t, rb, tiles['tm'],
                               F32 if tiles['moe_routed'] else BF16)
        if tiles['moe_routed']:
            n, tm_e = bsz * t, tiles['moe_tm']
            slot, tile_expert, n_valid, fill = _moe_plan(
                gates.reshape(n, LANE), counts[..., 0].reshape(-1, N_EXPERTS), tiles['tm'], tm_e)
            xs, slots = _moe_dispatch(h2.reshape(n, d), slot, fill, 2 * n + N_EXPERTS * tm_e, tiles['tm'], tm_e)
            y_rows = _moe_gemm(xs, tile_expert, n_valid, p['moe_w_gate'], p['moe_w_up'], p['moe_w_down'], l, tm_e)
            x = _moe_combine(y_rows, slots, gates, x1, mod, gfinal, final=(l == depth - 1))
        else:
            x = _moe(h2, gates, x1, mod, p['moe_w_gate'], p['moe_w_up'], p['moe_w_down'], gfinal, l,
                     tiles['moe_nb'], tiles['moe_tb'], final=(l == depth - 1))

        hs.append(s_h)
        rs.append(s5_state[:, :C_LANES].reshape(bsz, C_NGROUPS, C_STATE))
        ims.append(s5_state[:, C_LANES:].reshape(bsz, C_NGROUPS, C_STATE))
    cache5 = (depth, bsz, t, A_HEADS, LANE)
    return (x, k_all.reshape(cache5), v_all.reshape(cache5), jnp.stack(hs), jnp.stack(rs), jnp.stack(ims))


def kernel(x_prompt, x_sample, cache_k, cache_v, state_hgrn, state_s5_re, state_s5_im, c_prompt, c_sample, w_mod, b_mod, norm1_g, norm2_g, w_in, diff_lambda, diff_subln_g, hgrn_lb, hgrn_norm_g, s5_a_re, s5_a_im, s5_b_re, s5_b_im, s5_c_re, s5_c_im, s5_d, s5_log_dt, s5_w_glu, w_branch_a, w_branch_b, w_branch_c, w_out, router_w, router_b, moe_w_gate, moe_w_up, moe_w_down, final_norm_g):
    p = {
        'norm1_g': norm1_g, 'norm2_g': norm2_g, 'diff_lambda': diff_lambda, 'diff_subln_g': diff_subln_g,
        'hgrn_lb': hgrn_lb, 'hgrn_norm_g': hgrn_norm_g, 's5_a_re': s5_a_re, 's5_a_im': s5_a_im,
        's5_b_re': s5_b_re, 's5_b_im': s5_b_im, 's5_c_re': s5_c_re, 's5_c_im': s5_c_im, 's5_d': s5_d,
        's5_log_dt': s5_log_dt, 'router_w': router_w, 'router_b': router_b,
        'moe_w_gate': moe_w_gate, 'moe_w_up': moe_w_up, 'moe_w_down': moe_w_down, 'final_norm_g': final_norm_g,
    }
    wts = {'w_in': w_in.astype(BF16), 'w_glu': s5_w_glu.astype(BF16), 'w_a': w_branch_a.astype(BF16),
           'w_b': w_branch_b.astype(BF16), 'w_c': w_branch_c.astype(BF16), 'w_out': w_out.astype(BF16)}
    bp, tp, d = x_prompt.shape
    bs, ts, _ = x_sample.shape
    depth = w_mod.shape[0]
    mods = _modulation(jnp.concatenate([c_prompt, c_sample], axis=0), w_mod, b_mod)
    mods = mods.reshape(depth, bp + bs, 1, 6 * d)
    pos_p = jnp.arange(tp)
    pos_s = cache_k.shape[2] + jnp.arange(ts)
    tiles_p = dict(tm=min(256, tp), tq=min(256, tp), tb=min(512, tp), tc=min(64, tp),
                   moe_routed=True, moe_tm=256)
    tiles_s = dict(tm=ts, tq=ts, tb=ts, tc=ts, moe_routed=False, moe_nb=bs, moe_tb=ts)
    y_p, k_p, v_p, h_p, re_p, im_p = _trunk(x_prompt, mods[:, :bp], pos_p, p, wts, None, tiles_p)
    y_s, k_s, v_s, h_s, re_s, im_s = _trunk(x_sample, mods[:, bp:], pos_s, p, wts,
                                            (cache_k, cache_v, state_hgrn, state_s5_re, state_s5_im), tiles_s)
    return (y_p, y_s, k_p, v_p, h_p, re_p, im_p, k_s, v_s, h_s, re_s, im_s)
```

```python
import functools
import math

import numpy as np
import jax
import jax.numpy as jnp
from jax import lax
from jax.experimental import pallas as pl
from jax.experimental.pallas import tpu as pltpu

F32 = jnp.float32
BF16 = jnp.bfloat16

D_MODEL = 1024
CHUNK = 64
EPS = 1e-6
MASK_NEG = -1e30
ROPE_THETA = 10000.0
A_HEADS = 4
A_DH = 64
A_DV = 128
A_QK = 512
B_HEADS = 4
B_DK = 128
C_WIDTH = 512
C_GROUP = 16
C_NGROUPS = 32
C_STATE = 64
C_LANES = C_NGROUPS * C_STATE
N_EXPERTS = 16
N_EXPERT_GROUPS = 4
EXPERTS_PER_GROUP = 4
EXPERT_DFF = 512
IN_COLS = 7168
COL_AK, COL_AV, COL_B, COL_CU, COL_G = 512, 1024, 1536, 3584, 4096

LANE = 128
LOG2E = math.log2(math.e)
ONES_ROWS = 16
VMEM_LIMIT = 56 * 1024 * 1024


def _cparams(sem):
    return pltpu.CompilerParams(dimension_semantics=sem, vmem_limit_bytes=VMEM_LIMIT)


def _sigmoid(x):
    return 0.5 * jnp.tanh(0.5 * x) + 0.5


def _sigmoid_rel(x):
    return jax.nn.sigmoid(x)


def _dot(a, b):
    return jnp.dot(a, b, preferred_element_type=F32)


def _dot_nt(a, b):
    return lax.dot_general(a, b, (((1,), (1,)), ((), ())), preferred_element_type=F32)


def _dot_tn(a, b):
    return lax.dot_general(a, b, (((0,), (0,)), ((), ())), preferred_element_type=F32)


def _mod_kernel(c_ref, w_ref, b_ref, o_ref):
    c = c_ref[...]
    a = (c * _sigmoid(c)).astype(BF16)
    o_ref[...] = _dot(a, w_ref[...].astype(BF16)) + b_ref[...]


def _modulation(c_all, w_mod, b_mod):
    depth, d, n6 = w_mod.shape
    nb = c_all.shape[0]
    tn = 1536
    return pl.pallas_call(
        _mod_kernel,
        grid=(depth, n6 // tn),
        in_specs=[
            pl.BlockSpec((nb, d), lambda l, j: (0, 0)),
            pl.BlockSpec((None, d, tn), lambda l, j: (l, 0, j)),
            pl.BlockSpec((None, 1, tn), lambda l, j: (l, 0, j)),
        ],
        out_specs=pl.BlockSpec((None, nb, tn), lambda l, j: (l, 0, j)),
        out_shape=jax.ShapeDtypeStruct((depth, nb, n6), F32),
        compiler_params=_cparams(("parallel", "parallel")),
        name="modulation",
    )(c_all, w_mod, b_mod.reshape(depth, 1, n6))


def _swap_half(a):
    parts = []
    for j in range(a.shape[1] // LANE):
        s = a[:, LANE * j:LANE * (j + 1)]
        lane = lax.broadcasted_iota(jnp.int32, s.shape, 1)
        first = (lane & (A_DH - 1)) < (A_DH // 2)
        parts.append(jnp.where(first, pltpu.roll(s, LANE - A_DH // 2, 1), pltpu.roll(s, A_DH // 2, 1)))
    return jnp.concatenate(parts, axis=1)


def _in_kernel(x_ref, mod_ref, g_ref, w_ref, cos_ref, sin_ref, k_all_ref, v_all_ref,
               q_ref, kb_ref, vb_ref, k4_ref, v4_ref, b_ref, cu_ref, gt_ref):
    del k_all_ref, v_all_ref
    x = x_ref[...]
    ms = jnp.mean(x * x, axis=-1, keepdims=True)
    y = x * lax.rsqrt(ms + EPS) * g_ref[...]
    h = y * (1.0 + mod_ref[:, D_MODEL:2 * D_MODEL]) + mod_ref[:, 0:D_MODEL]
    hb = h.astype(BF16)

    def proj(c0, width):
        return _dot(hb, w_ref[:, c0:c0 + width])

    cos = cos_ref[...]
    sin = sin_ref[...]

    def rope(a):
        return a * cos + _swap_half(a) * sin

    q_ref[...] = (rope(proj(0, A_QK)) * (A_DH ** -0.5 * LOG2E)).astype(BF16)
    k = rope(proj(COL_AK, A_QK))
    v = proj(COL_AV, 512)
    kb_ref[...] = k.astype(BF16)
    vb_ref[...] = v.astype(BF16)
    tm = k.shape[0]
    for h in range(A_HEADS):
        k4_ref[pl.ds(h, tm, stride=A_HEADS), :] = k[:, h * LANE:(h + 1) * LANE]
        v4_ref[pl.ds(h, tm, stride=A_HEADS), :] = v[:, h * LANE:(h + 1) * LANE]
    for j in range(4):
        b_ref[:, 512 * j:512 * (j + 1)] = proj(COL_B + 512 * j, 512)
    cu_ref[...] = proj(COL_CU, C_WIDTH)
    for j in range(6):
        gt_ref[:, 512 * j:512 * (j + 1)] = proj(COL_G + 512 * j, 512)


def _in_projection(x, mod, g1, w_in, cos, sin, tm, layer, depth, k_all, v_all):
    bsz, t, d = x.shape
    nt = t // tm
    row = lambda b, i: (b, i, 0)
    cache_spec = pl.BlockSpec((None, None, tm * A_HEADS, LANE), lambda b, i: (layer, b, i, 0))
    cache_shape = jax.ShapeDtypeStruct((depth, bsz, t * A_HEADS, LANE), F32)
    in_specs = [
        pl.BlockSpec((None, tm, d), row),
        pl.BlockSpec((None, 1, 6 * d), lambda b, i: (b, 0, 0)),
        pl.BlockSpec((1, d), lambda b, i: (0, 0)),
        pl.BlockSpec((d, IN_COLS), lambda b, i: (0, 0), pipeline_mode=pl.Buffered(1)),
        pl.BlockSpec((tm, A_QK), lambda b, i: (i, 0)),
        pl.BlockSpec((tm, A_QK), lambda b, i: (i, 0)),
        pl.BlockSpec(memory_space=pl.ANY),
        pl.BlockSpec(memory_space=pl.ANY),
    ]
    args = [x, mod, g1, w_in, cos, sin, k_all, v_all]
    return pl.pallas_call(
        _in_kernel,
        grid=(bsz, nt),
        in_specs=in_specs,
        out_specs=[
            pl.BlockSpec((None, tm, A_QK), row),
            pl.BlockSpec((None, tm, A_QK), row),
            pl.BlockSpec((None, tm, 512), row),
            cache_spec,
            cache_spec,
            pl.BlockSpec((None, tm, 2048), row),
            pl.BlockSpec((tm, C_WIDTH), lambda b, i: (i, b)),
            pl.BlockSpec((None, tm, 3 * d), row),
        ],
        out_shape=[
            jax.ShapeDtypeStruct((bsz, t, A_QK), BF16),
            jax.ShapeDtypeStruct((bsz, t, A_QK), BF16),
            jax.ShapeDtypeStruct((bsz, t, 512), BF16),
            cache_shape,
            cache_shape,
            jax.ShapeDtypeStruct((bsz, t, 2048), F32),
            jax.ShapeDtypeStruct((t, bsz * C_WIDTH), F32),
            jax.ShapeDtypeStruct((bsz, t, 3 * d), F32),
        ],
        input_output_aliases={6: 3, 7: 4},
        compiler_params=_cparams(("parallel", "parallel")),
        name="in_projection",
    )(*args)


def _split_maps(q):
    lane = lax.broadcasted_iota(jnp.int32, q.shape, 1)
    zero = jnp.zeros_like(q)
    return jnp.concatenate([jnp.where(lane < A_DH, q, zero), jnp.where(lane >= A_DH, q, zero)], axis=0)


def _attn_finish(acc, l, lam, g, scale_out, tq):
    o = acc[:tq] / l[:tq] - lam * (acc[tq:] / l[tq:])
    ms = jnp.mean(o * o, axis=-1, keepdims=True)
    return (o * lax.rsqrt(ms + EPS) * g) * scale_out


def _attn_prompt_kernel(lam_ref, q_ref, k_ref, v_ref, g_ref, o_ref, vt, *, tq, scale_out):
    i = pl.program_id(1)
    nkv = vt.shape[0] // A_HEADS

    @pl.when(i == 0)
    def _():
        for h in range(A_HEADS):
            for jj in range(nkv):
                blk = (slice(jj * tq, (jj + 1) * tq), slice(h * LANE, (h + 1) * LANE))
                vt[h * nkv + jj] = jnp.concatenate(
                    [v_ref[blk].astype(F32).T.astype(BF16), jnp.ones((ONES_ROWS, tq), BF16)], axis=0)

    qq = [_split_maps(q_ref[:, h * LANE:(h + 1) * LANE]) for h in range(A_HEADS)]

    def scores(j, h):
        ks = k_ref[pl.ds(pl.multiple_of(j * tq, tq), tq), h * LANE:(h + 1) * LANE]
        return _dot_nt(ks, qq[h])

    def tile(j, carry, s, h, masked):
        m, acc = carry
        if masked:
            kr = lax.broadcasted_iota(jnp.int32, s.shape, 0)
            qc = lax.broadcasted_iota(jnp.int32, s.shape, 1)
            s = jnp.where((kr >> 6) <= ((qc & (tq - 1)) >> 6), s, MASK_NEG)
        mn = jnp.maximum(m, jnp.max(s, axis=0, keepdims=True))
        p = jnp.exp2((s - mn).astype(BF16))
        acc = jnp.exp2(m - mn) * acc + _dot(vt[h * nkv + j], p)
        return mn, acc

    def tiles(j, carries, masked):
        ss = [scores(j, h) for h in range(A_HEADS)]
        return tuple(tile(j, carries[h], ss[h], h, masked) for h in range(A_HEADS))

    init = (jnp.full((1, 2 * tq), MASK_NEG, F32), jnp.zeros((A_DV + ONES_ROWS, 2 * tq), F32))
    carries = lax.fori_loop(0, i, lambda j, c: tiles(j, c, False), (init,) * A_HEADS)
    carries = tiles(i, carries, True)
    for h in range(A_HEADS):
        acc = carries[h][1]
        acc = acc[:A_DV] / acc[A_DV:A_DV + 1]
        o = acc[:, :tq] - lam_ref[0:1, 0:1] * acc[:, tq:]
        ms = jnp.mean(o * o, axis=0, keepdims=True)
        y = (o * lax.rsqrt(ms + EPS) * g_ref[...]) * scale_out
        o_ref[:, h * LANE:(h + 1) * LANE] = y.T.astype(BF16)


def _attn_prompt(lam, q, k, v, g, scale_out, tq):
    bsz, t, _ = q.shape
    assert tq % CHUNK == 0 and tq & (tq - 1) == 0
    kern = functools.partial(_attn_prompt_kernel, tq=tq, scale_out=scale_out)
    return pl.pallas_call(
        kern,
        grid=(bsz, t // tq),
        in_specs=[
            pl.BlockSpec((1, LANE), lambda b, i: (0, 0)),
            pl.BlockSpec((None, tq, A_HEADS * LANE), lambda b, i: (b, i, 0)),
            pl.BlockSpec((None, t, A_HEADS * LANE), lambda b, i: (b, 0, 0)),
            pl.BlockSpec((None, t, A_HEADS * LANE), lambda b, i: (b, 0, 0)),
            pl.BlockSpec((A_DV, 1), lambda b, i: (0, 0)),
        ],
        out_specs=pl.BlockSpec((None, tq, A_HEADS * LANE), lambda b, i: (b, i, 0)),
        out_shape=jax.ShapeDtypeStruct((bsz, t, A_HEADS * A_DV), BF16),
        scratch_shapes=[pltpu.VMEM((A_HEADS * (t // tq), A_DV + ONES_ROWS, tq), BF16)],
        compiler_params=_cparams(("parallel", "arbitrary")),
        name="attn_prompt",
    )(lam, q, k, v, g.reshape(A_DV, 1))


def _attn_sample_kernel(lam_ref, q_ref, pk_ref, pv_ref, k_ref, v_ref, g_ref, o_ref, *, tq, scale_out):
    qq = _split_maps(q_ref[...])
    s_p = _dot_nt(qq, pk_ref[...].astype(BF16))
    s_n = _dot_nt(qq, k_ref[...])
    m = jnp.maximum(jnp.max(s_p, axis=-1, keepdims=True), jnp.max(s_n, axis=-1, keepdims=True))
    p_p = jnp.exp2(s_p - m)
    p_n = jnp.exp2(s_n - m)
    l = jnp.sum(p_p, axis=-1, keepdims=True) + jnp.sum(p_n, axis=-1, keepdims=True)
    acc = _dot(p_p.astype(BF16), pv_ref[...].astype(BF16)) + _dot(p_n.astype(BF16), v_ref[...])
    o_ref[...] = _attn_finish(acc, l, lam_ref[...], g_ref[...], scale_out, tq).astype(BF16)


def _attn_sample(lam, q, cache_k, cache_v, layer, k, v, g, scale_out):
    bsz, t, _ = q.shape
    past = cache_k.shape[2]
    kern = functools.partial(_attn_sample_kernel, tq=t, scale_out=scale_out)
    new = lambda b, h: (b, 0, h)
    old = lambda b, h: (layer, b, 0, h)
    return pl.pallas_call(
        kern,
        grid=(bsz, A_HEADS),
        in_specs=[
            pl.BlockSpec((1, LANE), lambda b, h: (0, 0)),
            pl.BlockSpec((None, t, LANE), new),
            pl.BlockSpec((None, None, past, LANE), old),
            pl.BlockSpec((None, None, past, LANE), old),
            pl.BlockSpec((None, t, LANE), new),
            pl.BlockSpec((None, t, LANE), new),
            pl.BlockSpec((1, LANE), lambda b, h: (0, 0)),
        ],
        out_specs=pl.BlockSpec((None, t, LANE), new),
        out_shape=jax.ShapeDtypeStruct((bsz, t, A_HEADS * A_DV), BF16),
        compiler_params=_cparams(("parallel", "parallel")),
        name="attn_sample",
    )(lam, q, cache_k, cache_v, k, v, g)


def _split3(x):
    hi = x.astype(BF16)
    r = x - hi.astype(F32)
    mid = r.astype(BF16)
    lo = (r - mid.astype(F32)).astype(BF16)
    return hi, mid, lo


HGRN_LEVELS = (5, 4, 3)
HGRN_DIAG = 8


def _hgrn_tables():
    L = CHUNK
    r = np.arange(L)[:, None]
    c = np.arange(L)[None, :]
    sgn, up, pair = [], [], []
    for sh in HGRN_LEVELS:
        upper = ((r >> sh) & 1) == 1
        up.append(np.broadcast_to(upper, (L, B_DK)))
        sgn.append(np.broadcast_to(np.where(upper, 1.0, -1.0), (L, B_DK)))
        pair.append((((r >> sh) & 1) == 1) & ((r >> (sh + 1)) == (c >> (sh + 1))) & (((c >> sh) & 1) == 0))
    diag = [(c == r - d) & ((r & (HGRN_DIAG - 1)) >= d) for d in range(HGRN_DIAG)]
    tril = (c <= r).astype(np.float32)
    rowm = jnp.asarray(np.stack(sgn + up).astype(np.float32))
    sqm = jnp.asarray(np.stack(pair + diag).astype(np.float32))
    tril3 = jnp.asarray(np.concatenate([tril, tril, tril], axis=1), dtype=BF16)
    return tril3, rowm, sqm


def _hgrn_chunk(heads, gn, tril3, rowm_ref, sqm_ref):
    L = CHUNK
    nl = len(HGRN_LEVELS)
    n = len(heads)

    fs, kfs, qs, bs = [], [], [], []
    for qraw, fraw, v, graw, lb, st in heads:
        f = lb + (1.0 - lb) * _sigmoid_rel(fraw)
        hi, mid, lo = _split3(jnp.log(f))
        fs.append(f)
        kfs.append(1.0 - f)
        qs.append(qraw * _sigmoid(qraw))
        bs.append(_dot(tril3, jnp.concatenate([hi, mid, lo], axis=0)))

    os_, atts, sts, vbs = [], [], [], []
    for i in range(n):
        qraw, fraw, v, graw, lb, st = heads[i]
        q, kf, b = qs[i], kfs[i], bs[i]
        vb = v.astype(BF16)
        o = _dot_nt((q * jnp.exp(b)).astype(BF16), st.astype(BF16))
        att = jnp.zeros((L, L), F32)
        for li, sh in enumerate(HGRN_LEVELS):
            m = 1 << sh
            ref_rows = [jnp.broadcast_to(b[(2 * j + 1) * m - 1:(2 * j + 1) * m, :], (2 * m, B_DK))
                        for j in range(L // (2 * m))]
            ref = ref_rows[0] if len(ref_rows) == 1 else jnp.concatenate(ref_rows, axis=0)
            w = jnp.exp((b - ref) * rowm_ref[li])
            wq = w * rowm_ref[nl + li]
            att = att + sqm_ref[li] * _dot_nt((q * wq).astype(BF16), (kf * (w - wq)).astype(BF16))
        b_last = b[L - 1:L, :]
        kdec = (kf * jnp.exp(b_last - b)).astype(BF16)
        sts.append(st * jnp.exp(b_last) + _dot_tn(vb, kdec))
        os_.append(o)
        atts.append(att)
        vbs.append(vb)

    outs = []
    for i in range(n):
        q, f, att = qs[i], fs[i], atts[i]
        hd = kfs[i]
        for d in range(HGRN_DIAG):
            if d > 0:
                hd = f * pltpu.roll(hd, 1, 0)
            att = att + sqm_ref[nl + d] * jnp.sum(q * hd, axis=-1, keepdims=True)
        outs.append(os_[i] + _dot(att.astype(BF16), vbs[i]))

    res = []
    for i in range(n):
        o, graw = outs[i], heads[i][3]
        ms = jnp.mean(o * o, axis=-1, keepdims=True)
        res.append((o * lax.rsqrt(ms + EPS) * gn * (graw * _sigmoid(graw)), sts[i]))
    return res


def _hgrn_kernel(*refs, nchunk, has_s0):
    if has_s0:
        q_ref, f_ref, v_ref, g_ref, lb_ref, gn_ref, tril_ref, rowm_ref, sqm_ref, s0_ref, y_ref, sf_ref, st = refs
    else:
        q_ref, f_ref, v_ref, g_ref, lb_ref, gn_ref, tril_ref, rowm_ref, sqm_ref, y_ref, sf_ref, st = refs
    c = pl.program_id(1)

    @pl.when(c == 0)
    def _():
        for h in range(B_HEADS):
            st[h] = s0_ref[h].T if has_s0 else jnp.zeros((LANE, B_DK), F32)

    gn = gn_ref[...]
    tril3 = tril_ref[...]

    def body(n, carry):
        sl = pl.ds(pl.multiple_of(n * CHUNK, CHUNK), CHUNK)
        hsl = [slice(h * LANE, (h + 1) * LANE) for h in range(B_HEADS)]
        heads = [(q_ref[sl, hs], f_ref[sl, hs], v_ref[sl, hs], g_ref[sl, hs], lb_ref[:, hs], st[h])
                 for h, hs in enumerate(hsl)]
        for h, (y, st_new) in enumerate(_hgrn_chunk(heads, gn, tril3, rowm_ref, sqm_ref)):
            st[h] = st_new
            y_ref[sl, hsl[h]] = y.astype(BF16)
        return carry

    lax.fori_loop(0, nchunk, body, 0, unroll=min(2, nchunk))

    @pl.when(c == pl.num_programs(1) - 1)
    def _():
        for h in range(B_HEADS):
            sf_ref[h] = st[h].T


def _hgrn(braw, lb, gn, s0, layer, tb):
    bsz, t, _ = braw.shape
    has_s0 = s0 is not None
    kern = functools.partial(_hgrn_kernel, nchunk=tb // CHUNK, has_s0=has_s0)
    width = B_HEADS * LANE
    tril3, rowm, sqm = _hgrn_tables()

    def col(j):
        return pl.BlockSpec((None, tb, width), lambda b, c: (b, c, j))

    def const(shape):
        return pl.BlockSpec(shape, lambda b, c: (0,) * len(shape))

    in_specs = [col(0), col(1), col(2), col(3), const((1, width)), const((1, LANE)),
                const(tril3.shape), const(rowm.shape), const(sqm.shape)]
    args = [braw, braw, braw, braw, lb, gn, tril3, rowm, sqm]
    if has_s0:
        in_specs.append(pl.BlockSpec((None, None, B_HEADS, B_DK, LANE), lambda b, c: (layer, b, 0, 0, 0)))
        args.append(s0)
    return pl.pallas_call(
        kern,
        grid=(bsz, t // tb),
        in_specs=in_specs,
        out_specs=[
            pl.BlockSpec((None, tb, width), lambda b, c: (b, c, 0)),
            pl.BlockSpec((None, B_HEADS, B_DK, LANE), lambda b, c: (b, 0, 0, 0)),
        ],
        out_shape=[
            jax.ShapeDtypeStruct((bsz, t, width), BF16),
            jax.ShapeDtypeStruct((bsz, B_HEADS, B_DK, LANE), F32),
        ],
        scratch_shapes=[pltpu.VMEM((B_HEADS, LANE, B_DK), F32)],
        compiler_params=_cparams(("parallel", "arbitrary")),
        name="hgrn",
    )(*args)


S5_COLS = 512
S5_BLOCK = 256


def _s5_kernel(u_ref, bm_ref, cm_ref, a_ref, d_ref, h0_ref, y_ref, hf_ref, xs, hs, *, tc, nb):
    step_id = pl.program_id(0)

    @pl.when(step_id == 0)
    def _():
        hs[...] = h0_ref[...]

    u = u_ref[...]
    ub = u.astype(BF16)
    for j in range(C_LANES // S5_BLOCK):
        q = (j * S5_BLOCK // C_STATE * C_GROUP) // LANE
        rows_k = slice(LANE * q, LANE * (q + 1))
        for part in (0, C_LANES):
            cols = slice(part + S5_BLOCK * j, part + S5_BLOCK * (j + 1))
            xs[:, cols] = _dot(ub[:, rows_k], bm_ref[rows_k, cols])
    for cc in range(C_LANES // S5_COLS):
        re = slice(cc * S5_COLS, (cc + 1) * S5_COLS)
        im = slice(C_LANES + cc * S5_COLS, C_LANES + (cc + 1) * S5_COLS)
        ar = a_ref[0:nb, re]
        ai = a_ref[nb:2 * nb, re]

        def step(t, carry):
            hr, hi = carry
            rows = pl.ds(pl.multiple_of(t * nb, nb), nb)
            nr = ar * hr - ai * hi + xs[rows, re]
            ni = ar * hi + ai * hr + xs[rows, im]
            xs[rows, re] = nr
            xs[rows, im] = ni
            return nr, ni

        hr, hi = lax.fori_loop(0, tc, step, (hs[:, re], hs[:, im]), unroll=8)
        hs[:, re] = hr
        hs[:, im] = hi
    for c in range(C_WIDTH // S5_BLOCK):
        ch = slice(S5_BLOCK * c, S5_BLOCK * (c + 1))
        n_st = S5_BLOCK // C_GROUP * C_STATE
        y = d_ref[:, ch] * u[:, ch]
        for part in (0, C_LANES):
            st = slice(part + n_st * c, part + n_st * (c + 1))
            y = y + _dot(xs[:, st].astype(BF16), cm_ref[st, ch])
        y_ref[:, ch] = y.astype(BF16)
    hf_ref[...] = hs[...]


def _s5(u_tm, bm, cm, a, d, h0, tc):
    n, _ = u_tm.shape
    nb = h0.shape[0]
    rows = tc * nb
    kern = functools.partial(_s5_kernel, tc=tc, nb=nb)
    const = lambda i: (0, 0)
    return pl.pallas_call(
        kern,
        grid=(n // rows,),
        in_specs=[
            pl.BlockSpec((rows, C_WIDTH), lambda i: (i, 0)),
            pl.BlockSpec((C_WIDTH, 2 * C_LANES), const, pipeline_mode=pl.Buffered(1)),
            pl.BlockSpec((2 * C_LANES, C_WIDTH), const, pipeline_mode=pl.Buffered(1)),
            pl.BlockSpec((2 * nb, C_LANES), const),
            pl.BlockSpec((1, C_WIDTH), const),
            pl.BlockSpec((nb, 2 * C_LANES), const),
        ],
        out_specs=[
            pl.BlockSpec((rows, C_WIDTH), lambda i: (i, 0)),
            pl.BlockSpec((nb, 2 * C_LANES), const),
        ],
        out_shape=[
            jax.ShapeDtypeStruct((n, C_WIDTH), BF16),
            jax.ShapeDtypeStruct((nb, 2 * C_LANES), F32),
        ],
        scratch_shapes=[pltpu.VMEM((rows, 2 * C_LANES), F32), pltpu.VMEM((nb, 2 * C_LANES), F32)],
        compiler_params=_cparams(("arbitrary",)),
        name="s5",
    )(u_tm, bm, cm, a, d, h0)


def _s5_tables(a_re, a_im, b_re, b_im, c_re, c_im, log_dt, nb):
    dt = jnp.exp(log_dt)[:, None]
    mag = jnp.exp(a_re * dt)
    abr, abi = mag * jnp.cos(a_im * dt), mag * jnp.sin(a_im * dt)
    den = a_re * a_re + a_im * a_im
    coef_r = ((abr - 1.0) * a_re + abi * a_im) / den
    coef_i = (abi * a_re - (abr - 1.0) * a_im) / den
    fr = coef_r[..., None] * b_re - coef_i[..., None] * b_im
    fi = coef_r[..., None] * b_im + coef_i[..., None] * b_re
    eye = jnp.eye(C_NGROUPS, dtype=F32)

    def blockdiag_in(m):
        return jnp.einsum('gpj,gh->gjhp', m, eye).reshape(C_WIDTH, C_LANES)

    def blockdiag_out(m):
        return jnp.einsum('gjp,gh->gphj', m, eye).reshape(C_LANES, C_WIDTH)

    bm = jnp.concatenate([blockdiag_in(fr), blockdiag_in(fi)], axis=1).astype(BF16)
    cm = jnp.concatenate([blockdiag_out(c_re), -blockdiag_out(c_im)], axis=0).astype(BF16)
    a = jnp.concatenate([jnp.broadcast_to(abr.reshape(1, C_LANES), (nb, C_LANES)),
                         jnp.broadcast_to(abi.reshape(1, C_LANES), (nb, C_LANES))], axis=0)
    return bm, cm, a


def _pair_max_sum(v):
    best = v[0:1] + v[1:2]
    for i, j in ((0, 2), (0, 3), (1, 2), (1, 3), (2, 3)):
        best = jnp.maximum(best, v[i:i + 1] + v[j:j + 1])
    return best


def _route(logits_t, rb):
    mx = jnp.max(logits_t, axis=0, keepdims=True)
    ex = jnp.exp(logits_t - mx)
    s = ex / jnp.sum(ex, axis=0, keepdims=True)
    sb = s + rb
    g = EXPERTS_PER_GROUP
    best = _pair_max_sum(sb[0:g])
    sel = jnp.zeros(best.shape, jnp.int32)
    for gi in range(1, N_EXPERT_GROUPS):
        sc = _pair_max_sum(sb[gi * g:(gi + 1) * g])
        take = sc > best
        best = jnp.where(take, sc, best)
        sel = jnp.where(take, gi, sel)
    sb4 = jnp.zeros((g,) + best.shape[1:], F32)
    s4 = jnp.zeros((g,) + best.shape[1:], F32)
    for gi in range(N_EXPERT_GROUPS):
        on = sel == gi
        sb4 = jnp.where(on, sb[gi * g:(gi + 1) * g], sb4)
        s4 = jnp.where(on, s[gi * g:(gi + 1) * g], s4)
    rows, keeps = [], []
    for e in range(g):
        rank = jnp.zeros(best.shape, jnp.int32)
        for j in range(g):
            if j == e:
                continue
            ahead = (sb4[j:j + 1] > sb4[e:e + 1]) if j > e else (sb4[j:j + 1] >= sb4[e:e + 1])
            rank = rank + ahead.astype(jnp.int32)
        rows.append(jnp.where(rank < 2, s4[e:e + 1], 0.0))
        keeps.append(jnp.where(rank < 2, 1.0, 0.0))
    w4 = jnp.concatenate(rows, axis=0)
    w4 = w4 / jnp.sum(w4, axis=0, keepdims=True)
    keep4 = jnp.concatenate(keeps, axis=0)
    gates = jnp.concatenate([jnp.where(sel == gi, w4, 0.0) for gi in range(N_EXPERT_GROUPS)], axis=0)
    chosen = jnp.concatenate([jnp.where(sel == gi, keep4, 0.0) for gi in range(N_EXPERT_GROUPS)], axis=0) > 0.5
    eid = lax.broadcasted_iota(jnp.int32, gates.shape, 0).astype(F32)
    e_lo = jnp.min(jnp.where(chosen, eid, float(N_EXPERTS)), axis=0, keepdims=True)
    e_hi = jnp.max(jnp.where(chosen, eid, -1.0), axis=0, keepdims=True)
    w_lo = jnp.sum(jnp.where(eid == e_lo, gates, 0.0), axis=0, keepdims=True)
    w_hi = jnp.sum(jnp.where(eid == e_hi, gates, 0.0), axis=0, keepdims=True)
    n = gates.shape[1]
    tri = jnp.where(lax.broadcasted_iota(jnp.int32, (n, n), 0) <= lax.broadcasted_iota(jnp.int32, (n, n), 1),
                    1.0, 0.0).astype(BF16)
    chosen_f = jnp.where(chosen, 1.0, 0.0)
    cum = _dot(chosen_f.astype(BF16), tri)
    before = cum - chosen_f
    r_lo = jnp.sum(jnp.where(eid == e_lo, before, 0.0), axis=0, keepdims=True)
    r_hi = jnp.sum(jnp.where(eid == e_hi, before, 0.0), axis=0, keepdims=True)
    return gates, jnp.concatenate([e_lo, e_hi, w_lo, w_hi, r_lo, r_hi], axis=0), cum[:, n - 1:n]


ROW_TILES = D_MODEL // LANE


def _to_row_tiles(ref, val):
    for j in range(ROW_TILES):
        ref[:, j, :] = val[:, j * LANE:(j + 1) * LANE]


def _from_row_tiles(ref):
    return jnp.concatenate([ref[:, j, :] for j in range(ROW_TILES)], axis=1)


def _merge_kernel(x_ref, ya_ref, yb_ref, ys_ref, gt_ref, mod_ref, g2_ref,
                  wglu_ref, wa_ref, wb_ref, wc_ref, wout_ref, rw_ref, rb_ref,
                  x1_ref, h2_ref, gates_ref, counts_ref):
    d = D_MODEL
    z = _dot(ys_ref[...], wglu_ref[...])
    yc = (z[:, :C_WIDTH] * _sigmoid(z[:, C_WIDTH:])).astype(BF16)
    merged = (_sigmoid(gt_ref[:, 0:d]) * _dot(ya_ref[...], wa_ref[...])
              + _sigmoid(gt_ref[:, d:2 * d]) * _dot(yb_ref[...], wb_ref[...])
              + _sigmoid(gt_ref[:, 2 * d:3 * d]) * _dot(yc, wc_ref[...]))
    mix = _dot(merged.astype(BF16), wout_ref[...])
    x1 = x_ref[...] + mod_ref[:, 2 * d:3 * d] * mix
    x1_ref[...] = x1
    ms = jnp.mean(x1 * x1, axis=-1, keepdims=True)
    h2 = (x1 * lax.rsqrt(ms + EPS) * g2_ref[...]) * (1.0 + mod_ref[:, 4 * d:5 * d]) + mod_ref[:, 3 * d:4 * d]
    h_hi, h_mid, h_lo = _split3(h2)
    if len(h2_ref.shape) == 3:
        _to_row_tiles(h2_ref, h_hi.astype(F32))
    else:
        h2_ref[...] = h_hi
    hcat = jnp.concatenate([h_hi, h_mid, h_hi, h_lo, h_mid, h_hi], axis=1)
    gates_t, picks_t, counts = _route(_dot_nt(rw_ref[...], hcat), rb_ref[...])
    pad = jnp.zeros((LANE - N_EXPERTS - picks_t.shape[0], gates_t.shape[1]), F32)
    gates_ref[...] = jnp.concatenate([gates_t, picks_t, pad], axis=0).T
    counts_ref[...] = jnp.broadcast_to(counts, counts_ref.shape)


def _merge(x, ya, yb, ys_tm, graw, mod, g2, wglu, wa, wb, wc, wout, rw_t, rb, tm, h2_row_tiles):
    bsz, t, d = x.shape
    row = lambda b, i: (b, i, 0)
    const = lambda b, i: (0, 0)
    if h2_row_tiles:
        h2_spec = pl.BlockSpec((None, tm, ROW_TILES, LANE), lambda b, i: (b, i, 0, 0))
        h2_shape = jax.ShapeDtypeStruct((bsz, t, ROW_TILES, LANE), F32)
    else:
        h2_spec = pl.BlockSpec((None, tm, d), row)
        h2_shape = jax.ShapeDtypeStruct((bsz, t, d), BF16)

    def wspec(shape):
        return pl.BlockSpec(shape, const, pipeline_mode=pl.Buffered(1))

    return pl.pallas_call(
        _merge_kernel,
        grid=(bsz, t // tm),
        in_specs=[
            pl.BlockSpec((None, tm, d), row),
            pl.BlockSpec((None, tm, 512), row),
            pl.BlockSpec((None, tm, 512), row),
            pl.BlockSpec((tm, C_WIDTH), lambda b, i: (i, b)),
            pl.BlockSpec((None, tm, 3 * d), row),
            pl.BlockSpec((None, 1, 6 * d), lambda b, i: (b, 0, 0)),
            pl.BlockSpec((1, d), const),
            wspec((C_WIDTH, 2 * C_WIDTH)), wspec((512, d)), wspec((512, d)), wspec((C_WIDTH, d)),
            wspec((d, d)), wspec((N_EXPERTS, 6 * d)), wspec((N_EXPERTS, 1)),
        ],
        out_specs=[
            pl.BlockSpec((None, tm, d), row),
            h2_spec,
            pl.BlockSpec((None, tm, LANE), row),
            pl.BlockSpec((None, None, N_EXPERTS, LANE), lambda b, i: (b, i, 0, 0)),
        ],
        out_shape=[
            jax.ShapeDtypeStruct((bsz, t, d), F32),
            h2_shape,
            jax.ShapeDtypeStruct((bsz, t, LANE), F32),
            jax.ShapeDtypeStruct((bsz, t // tm, N_EXPERTS, LANE), F32),
        ],
        compiler_params=_cparams(("parallel", "parallel")),
        name="merge_router",
    )(x, ya, yb, ys_tm, graw, mod, g2, wglu, wa, wb, wc, wout, rw_t, rb)


def _moe_kernel(h_ref, gates_ref, x1_ref, mod_ref, wg_ref, wu_ref, wd_ref, gf_ref, o_ref, acc, *, final):
    e = pl.program_id(1)
    nbt, tb, d = h_ref.shape
    tm = nbt * tb

    @pl.when(e == 0)
    def _():
        acc[...] = jnp.zeros_like(acc)

    h = h_ref[...].reshape(tm, d)
    a = _dot(h, wg_ref[...].astype(BF16))
    u = _dot(h, wu_ref[...].astype(BF16))
    gates = gates_ref[...].reshape(tm, LANE)
    lane = lax.broadcasted_iota(jnp.int32, gates.shape, 1)
    gcol = jnp.sum(jnp.where(lane == e, gates, 0.0), axis=-1, keepdims=True)
    hid = (a * _sigmoid(a)) * u * gcol
    acc[...] += _dot(hid.astype(BF16), wd_ref[...].astype(BF16))

    @pl.when(e == pl.num_programs(1) - 1)
    def _():
        x2 = x1_ref[...] + mod_ref[:, :, 5 * d:6 * d] * acc[...].reshape(nbt, tb, d)
        if final:
            ms = jnp.mean(x2 * x2, axis=-1, keepdims=True)
            x2 = x2 * lax.rsqrt(ms + EPS) * gf_ref[...]
        o_ref[...] = x2


def _moe(h2, gates, x1, mod, wg, wu, wd, gfinal, layer, nbt, tb, final):
    bsz, t, d = x1.shape
    kern = functools.partial(_moe_kernel, final=final)
    tok = lambda i, e: (i // (t // tb), i % (t // tb), 0) if nbt == 1 else (i, 0, 0)
    nsteps = (bsz // nbt) * (t // tb)
    return pl.pallas_call(
        kern,
        grid=(nsteps, N_EXPERTS),
        in_specs=[
            pl.BlockSpec((nbt, tb, d), tok),
            pl.BlockSpec((nbt, tb, LANE), tok),
            pl.BlockSpec((nbt, tb, d), tok),
            pl.BlockSpec((nbt, 1, 6 * d), lambda i, e: ((i // (t // tb)) if nbt == 1 else i, 0, 0)),
            pl.BlockSpec((None, None, d, EXPERT_DFF), lambda i, e: (layer, e, 0, 0)),
            pl.BlockSpec((None, None, d, EXPERT_DFF), lambda i, e: (layer, e, 0, 0)),
            pl.BlockSpec((None, None, EXPERT_DFF, d), lambda i, e: (layer, e, 0, 0)),
            pl.BlockSpec((1, d), lambda i, e: (0, 0)),
        ],
        out_specs=pl.BlockSpec((nbt, tb, d), tok),
        out_shape=jax.ShapeDtypeStruct((bsz, t, d), F32),
        scratch_shapes=[pltpu.VMEM((nbt * tb, d), F32)],
        compiler_params=_cparams(("parallel", "arbitrary")),
        name="moe",
    )(h2, gates, x1, mod, wg, wu, wd, gfinal)


def _moe_plan(info, counts, tok_tile, tm):
    n = info.shape[0]
    n_tiles = (2 * n) // tm + N_EXPERTS
    counts = counts.astype(jnp.int32)
    total = jnp.sum(counts, axis=0)
    padded = ((total + tm - 1) // tm) * tm
    ends = jnp.cumsum(padded)
    offs = ends - padded
    tile_base = offs[None, :] + jnp.cumsum(counts, axis=0) - counts
    base_tok = jnp.repeat(tile_base, tok_tile, axis=0)
    experts = jnp.arange(N_EXPERTS, dtype=jnp.int32)[None, :]
    slots = []
    for k in range(2):
        e = info[:, N_EXPERTS + k].astype(jnp.int32)
        r = info[:, N_EXPERTS + 4 + k].astype(jnp.int32)
        slots.append(jnp.sum(jnp.where(e[:, None] == experts, base_tok, 0), axis=1) + r)
    starts = jnp.arange(n_tiles, dtype=jnp.int32) * tm
    tile_expert = jnp.minimum(jnp.sum((starts[:, None] >= ends[None, :]).astype(jnp.int32), axis=1), N_EXPERTS - 1)
    fill = jnp.stack([offs + total, ends]).astype(jnp.int32)
    return jnp.stack(slots, axis=1), tile_expert.astype(jnp.int32), (ends[-1:] // tm).astype(jnp.int32), fill


def _row_wait(count, src, dst, sem):
    def body(r, carry):
        pltpu.make_async_copy(src.at[0], dst.at[0], sem).wait()
        return carry
    lax.fori_loop(0, count, body, 0, unroll=8)


def _moe_dispatch_kernel(fill_ref, sl_ref, h_ref, xs_hbm, zrow, sems, *, n_rows, tm):
    i = pl.program_id(0)
    sem = sems.at[0]
    td = h_ref.shape[0]
    n_pad = N_EXPERTS * tm

    @pl.when(i == 0)
    def _():
        zrow[...] = jnp.zeros_like(zrow)

        def zero_rows(lo, hi):
            def body(s, carry):
                pltpu.make_async_copy(zrow.at[0], xs_hbm.at[s], sem).start()
                return carry
            lax.fori_loop(lo, hi, body, 0)

        for e in range(N_EXPERTS):
            zero_rows(fill_ref[0, e], fill_ref[1, e])
        zero_rows(fill_ref[1, N_EXPERTS - 1], n_rows)
        _row_wait(n_pad, zrow, xs_hbm, sem)

    for r in range(td):
        for k in range(2):
            pltpu.make_async_copy(h_ref.at[r], xs_hbm.at[sl_ref[0, k * td + r]], sem).start(priority=k)
    _row_wait(2 * td, h_ref, xs_hbm, sem)


def _moe_dispatch(h2, slot, fill, n_rows, td, tm):
    n = h2.shape[0]
    nt = n // td
    slots = slot.reshape(nt, td, 2).transpose(0, 2, 1).reshape(nt, 1, 2 * td)
    grid_spec = pltpu.PrefetchScalarGridSpec(
        num_scalar_prefetch=1,
        grid=(nt,),
        in_specs=[
            pl.BlockSpec((None, 1, 2 * td), lambda i, fl: (i, 0, 0), memory_space=pltpu.SMEM),
            pl.BlockSpec((td, ROW_TILES, LANE), lambda i, fl: (i, 0, 0)),
        ],
        out_specs=pl.BlockSpec(memory_space=pl.ANY),
        scratch_shapes=[pltpu.VMEM((1, ROW_TILES, LANE), F32), pltpu.SemaphoreType.DMA((1,))],
    )
    return pl.pallas_call(
        functools.partial(_moe_dispatch_kernel, n_rows=n_rows, tm=tm),
        grid_spec=grid_spec,
        out_shape=jax.ShapeDtypeStruct((n_rows, ROW_TILES, LANE), F32),
        compiler_params=_cparams(("arbitrary",)),
        name="moe_dispatch",
    )(fill, slots, h2), slots


def _moe_gemm_kernel(te_ref, nv_ref, x_ref, wg_ref, wu_ref, wd_ref, y_ref):
    i = pl.program_id(0)
    nv = nv_ref[0]

    @pl.when(i < nv)
    def _():
        x = _from_row_tiles(x_ref).astype(BF16)
        a = _dot(x, wg_ref[...].astype(BF16))
        u = _dot(x, wu_ref[...].astype(BF16))
        _to_row_tiles(y_ref, _dot(((a * _sigmoid(a)) * u).astype(BF16), wd_ref[...].astype(BF16)))

    @pl.when(i >= nv)
    def _():
        y_ref[...] = jnp.zeros_like(y_ref)


def _moe_gemm(xs, tile_expert, n_valid, wg, wu, wd, layer, tm):
    n_rows, d = xs.shape[0], D_MODEL
    n_tiles = n_rows // tm
    rows = pl.BlockSpec((tm, ROW_TILES, LANE), lambda i, te, nv: (i, 0, 0))
    grid_spec = pltpu.PrefetchScalarGridSpec(
        num_scalar_prefetch=2,
        grid=(n_tiles,),
        in_specs=[
            rows,
            pl.BlockSpec((None, None, d, EXPERT_DFF), lambda i, te, nv: (layer, te[i], 0, 0)),
            pl.BlockSpec((None, None, d, EXPERT_DFF), lambda i, te, nv: (layer, te[i], 0, 0)),
            pl.BlockSpec((None, None, EXPERT_DFF, d), lambda i, te, nv: (layer, te[i], 0, 0)),
        ],
        out_specs=rows,
    )
    return pl.pallas_call(
        _moe_gemm_kernel,
        grid_spec=grid_spec,
        out_shape=jax.ShapeDtypeStruct((n_rows, ROW_TILES, LANE), F32),
        compiler_params=_cparams(("arbitrary",)),
        name="moe_gemm",
    )(tile_expert, n_valid, xs, wg, wu, wd)


def _moe_combine_kernel(sl_cur, sl_nxt, y_hbm, info_ref, x1_ref, mod_ref, gf_ref, o_ref, ybuf, sems, *, final):
    i = pl.program_id(0)
    n = pl.num_programs(0)
    tc = info_ref.shape[0]
    d = D_MODEL
    cur = lax.rem(i, 2)

    def gather(sl_ref, buf):
        for q in range(2 * tc):
            pltpu.make_async_copy(y_hbm.at[sl_ref[0, q]], ybuf.at[buf, q],
                                  sems.at[buf]).start(priority=q % 2)

    @pl.when(i == 0)
    def _():
        gather(sl_cur, 0)

    for buf in range(2):
        @pl.when((i + 1 < n) & (cur == 1 - buf))
        def _():
            gather(sl_nxt, buf)

    _row_wait(2 * tc, y_hbm, ybuf.at[cur], sems.at[cur])
    w_lo = info_ref[:, N_EXPERTS + 2:N_EXPERTS + 3]
    w_hi = info_ref[:, N_EXPERTS + 3:N_EXPERTS + 4]
    moe = (w_lo * _from_row_tiles(ybuf.at[cur, pl.ds(0, tc)])
           + w_hi * _from_row_tiles(ybuf.at[cur, pl.ds(tc, tc)]))
    x2 = x1_ref[...] + mod_ref[:, 5 * d:6 * d] * moe
    if final:
        ms = jnp.mean(x2 * x2, axis=-1, keepdims=True)
        x2 = x2 * lax.rsqrt(ms + EPS) * gf_ref[...]
    o_ref[...] = x2


def _moe_combine(y, slots, info, x1, mod, gfinal, final):
    bsz, t, d = x1.shape
    n = bsz * t
    nt = slots.shape[0]
    tc = slots.shape[2] // 2
    per_b = t // tc
    row = lambda i: (i, 0)
    out = pl.pallas_call(
        functools.partial(_moe_combine_kernel, final=final),
        grid=(nt,),
        in_specs=[
            pl.BlockSpec((None, 1, 2 * tc), lambda i: (i, 0, 0), memory_space=pltpu.SMEM),
            pl.BlockSpec((None, 1, 2 * tc), lambda i: (jnp.minimum(i + 1, nt - 1), 0, 0), memory_space=pltpu.SMEM),
            pl.BlockSpec(memory_space=pl.ANY),
            pl.BlockSpec((tc, LANE), row),
            pl.BlockSpec((tc, d), row),
            pl.BlockSpec((None, 1, 6 * d), lambda i: (i // per_b, 0, 0)),
            pl.BlockSpec((1, d), lambda i: (0, 0)),
        ],
        out_specs=pl.BlockSpec((tc, d), row),
        out_shape=jax.ShapeDtypeStruct((n, d), F32),
        scratch_shapes=[pltpu.VMEM((2, 2 * tc, ROW_TILES, LANE), F32), pltpu.SemaphoreType.DMA((2,))],
        compiler_params=_cparams(("arbitrary",)),
        name="moe_combine",
    )(slots, slots, y, info.reshape(n, LANE), x1.reshape(n, d), mod, gfinal)
    return out.reshape(bsz, t, d)


def _rope_tables(pos):
    inv = ROPE_THETA ** (-jnp.arange(0, A_DH, 2, dtype=F32) / A_DH)
    ang = pos.astype(F32)[:, None] * inv[None, :]
    cos, sin = jnp.cos(ang), jnp.sin(ang)
    seg_cos = jnp.concatenate([cos, cos], axis=1)
    seg_sin = jnp.concatenate([-sin, sin], axis=1)
    reps = A_QK // A_DH
    return jnp.tile(seg_cos, (1, reps)), jnp.tile(seg_sin, (1, reps))


def _trunk(x, mods, pos, p, wts, past, tiles):
    bsz, t, d = x.shape
    depth = mods.shape[0]
    cos, sin = _rope_tables(pos)
    lb_p = jax.nn.softmax(p['hgrn_lb'].astype(F32), axis=0)
    lbs = jnp.cumsum(lb_p, axis=0) - lb_p[0:1]
    w_hi, w_mid, w_lo = _split3(p['router_w'].T.astype(F32))
    rw_t = jnp.concatenate([w_hi, w_hi, w_mid, w_hi, w_mid, w_lo], axis=1)
    rb = p['router_b'].reshape(N_EXPERTS, 1).astype(F32)
    gfinal = p['final_norm_g'].reshape(1, d)
    hs, rs, ims = [], [], []
    k_all = jnp.zeros((depth, bsz, t * A_HEADS, LANE), F32)
    v_all = jnp.zeros((depth, bsz, t * A_HEADS, LANE), F32)
    for l in range(depth):
        mod = mods[l]
        q, k, v, k_all, v_all, braw, cu_tm, graw = _in_projection(
            x, mod, p['norm1_g'][l].reshape(1, d), wts['w_in'][l], cos, sin, tiles['tm'], l, depth, k_all, v_all)

        lam_init = 0.8 - 0.6 * math.exp(-0.3 * l)
        lp = p['diff_lambda'][l].astype(F32)
        lam = jnp.exp(jnp.sum(lp[0] * lp[1])) - jnp.exp(jnp.sum(lp[2] * lp[3])) + lam_init
        lam_row = jnp.full((1, LANE), lam, F32)
        subg = p['diff_subln_g'][l].reshape(1, A_DV)
        if past is None:
            ya = _attn_prompt(lam_row, q, k, v, subg, 1.0 - lam_init, tiles['tq'])
        else:
            pk = past[0].reshape(depth, bsz, -1, A_HEADS * 2 * A_DH)
            pv = past[1].reshape(depth, bsz, -1, A_HEADS * A_DV)
            ya = _attn_sample(lam_row, q, pk, pv, l, k, v, subg, 1.0 - lam_init)

        yb, s_h = _hgrn(braw, lbs[l].reshape(1, B_HEADS * B_DK), p['hgrn_norm_g'][l].reshape(1, LANE),
                        None if past is None else past[2], l, tiles['tb'])

        bm, cm, a_tab = _s5_tables(p['s5_a_re'][l], p['s5_a_im'][l], p['s5_b_re'][l], p['s5_b_im'][l],
                                   p['s5_c_re'][l], p['s5_c_im'][l], p['s5_log_dt'][l], bsz)
        if past is None:
            h0 = jnp.zeros((bsz, 2 * C_LANES), F32)
        else:
            h0 = jnp.concatenate([past[3][l].reshape(bsz, C_LANES), past[4][l].reshape(bsz, C_LANES)], axis=1)
        ys_tm, s5_state = _s5(cu_tm.reshape(t * bsz, C_WIDTH), bm, cm, a_tab,
                              p['s5_d'][l].reshape(1, C_WIDTH), h0, tiles['tc'])

        x1, h2, gates, counts = _merge(x, ya, yb, ys_tm.reshape(t, bsz * C_WIDTH), graw, mod,
                               p['norm2_g'][l].reshape(1, d), wts['w_glu'][l], wts['w_a'][l], wts['w_b'][l],
                               wts['w_c'][l], wts['w_out'][l], rw_t, rb, tiles['tm_merge'],
                               tiles['moe_routed'])
        if tiles['moe_routed']:
            n, tm_e = bsz * t, tiles['moe_tm']
            slot, tile_expert, n_valid, fill = _moe_plan(
                gates.reshape(n, LANE), counts[..., 0].reshape(-1, N_EXPERTS), tiles['tm_merge'], tm_e)
            xs, slots = _moe_dispatch(h2.reshape(n, ROW_TILES, LANE), slot, fill, 2 * n + N_EXPERTS * tm_e, tiles['tm'], tm_e)
            y_rows = _moe_gemm(xs, tile_expert, n_valid, p['moe_w_gate'], p['moe_w_up'], p['moe_w_down'], l, tm_e)
            x = _moe_combine(y_rows, slots, gates, x1, mod, gfinal, final=(l == depth - 1))
        else:
            x = _moe(h2, gates, x1, mod, p['moe_w_gate'], p['moe_w_up'], p['moe_w_down'], gfinal, l,
                     tiles['moe_nb'], tiles['moe_tb'], final=(l == depth - 1))

        hs.append(s_h)
        rs.append(s5_state[:, :C_LANES].reshape(bsz, C_NGROUPS, C_STATE))
        ims.append(s5_state[:, C_LANES:].reshape(bsz, C_NGROUPS, C_STATE))
    cache5 = (depth, bsz, t, A_HEADS, LANE)
    return (x, k_all.reshape(cache5), v_all.reshape(cache5), jnp.stack(hs), jnp.stack(rs), jnp.stack(ims))


def kernel(x_prompt, x_sample, cache_k, cache_v, state_hgrn, state_s5_re, state_s5_im, c_prompt, c_sample, w_mod, b_mod, norm1_g, norm2_g, w_in, diff_lambda, diff_subln_g, hgrn_lb, hgrn_norm_g, s5_a_re, s5_a_im, s5_b_re, s5_b_im, s5_c_re, s5_c_im, s5_d, s5_log_dt, s5_w_glu, w_branch_a, w_branch_b, w_branch_c, w_out, router_w, router_b, moe_w_gate, moe_w_up, moe_w_down, final_norm_g):
    p = {
        'norm1_g': norm1_g, 'norm2_g': norm2_g, 'diff_lambda': diff_lambda, 'diff_subln_g': diff_subln_g,
        'hgrn_lb': hgrn_lb, 'hgrn_norm_g': hgrn_norm_g, 's5_a_re': s5_a_re, 's5_a_im': s5_a_im,
        's5_b_re': s5_b_re, 's5_b_im': s5_b_im, 's5_c_re': s5_c_re, 's5_c_im': s5_c_im, 's5_d': s5_d,
        's5_log_dt': s5_log_dt, 'router_w': router_w, 'router_b': router_b,
        'moe_w_gate': moe_w_gate, 'moe_w_up': moe_w_up, 'moe_w_down': moe_w_down, 'final_norm_g': final_norm_g,
    }
    wts = {'w_in': w_in.astype(BF16), 'w_glu': s5_w_glu.astype(BF16), 'w_a': w_branch_a.astype(BF16),
           'w_b': w_branch_b.astype(BF16), 'w_c': w_branch_c.astype(BF16), 'w_out': w_out.astype(BF16)}
    bp, tp, d = x_prompt.shape
    bs, ts, _ = x_sample.shape
    depth = w_mod.shape[0]
    mods = _modulation(jnp.concatenate([c_prompt, c_sample], axis=0), w_mod, b_mod)
    mods = mods.reshape(depth, bp + bs, 1, 6 * d)
    pos_p = jnp.arange(tp)
    pos_s = cache_k.shape[2] + jnp.arange(ts)
    tiles_p = dict(tm=min(256, tp), tm_merge=min(512, tp), tq=min(256, tp), tb=min(512, tp), tc=min(64, tp),
                   moe_routed=True, moe_tm=256)
    tiles_s = dict(tm=ts, tm_merge=ts, tq=ts, tb=ts, tc=ts, moe_routed=False, moe_nb=bs, moe_tb=ts)
    y_p, k_p, v_p, h_p, re_p, im_p = _trunk(x_prompt, mods[:, :bp], pos_p, p, wts, None, tiles_p)
    y_s, k_s, v_s, h_s, re_s, im_s = _trunk(x_sample, mods[:, bp:], pos_s, p, wts,
                                            (cache_k, cache_v, state_hgrn, state_s5_re, state_s5_im), tiles_s)
    return (y_p, y_s, k_p, v_p, h_p, re_p, im_p, k_s, v_s, h_s, re_s, im_s)
```

```python
import functools
import math

import numpy as np
import jax
import jax.numpy as jnp
from jax import lax
from jax.experimental import pallas as pl
from jax.experimental.pallas import tpu as pltpu

F32 = jnp.float32
BF16 = jnp.bfloat16

D_MODEL = 1024
CHUNK = 64
EPS = 1e-6
MASK_NEG = -1e30
ROPE_THETA = 10000.0
A_HEADS = 4
A_DH = 64
A_DV = 128
A_QK = 512
B_HEADS = 4
B_DK = 128
C_WIDTH = 512
C_GROUP = 16
C_NGROUPS = 32
C_STATE = 64
C_LANES = C_NGROUPS * C_STATE
N_EXPERTS = 16
N_EXPERT_GROUPS = 4
EXPERTS_PER_GROUP = 4
EXPERT_DFF = 512
IN_COLS = 7168
COL_AK, COL_AV, COL_B, COL_CU, COL_G = 512, 1024, 1536, 3584, 4096

LANE = 128
LOG2E = math.log2(math.e)
ONES_ROWS = 16
VMEM_LIMIT = 56 * 1024 * 1024


def _cparams(sem):
    return pltpu.CompilerParams(dimension_semantics=sem, vmem_limit_bytes=VMEM_LIMIT)


def _sigmoid(x):
    return 0.5 * jnp.tanh(0.5 * x) + 0.5


def _sigmoid_rel(x):
    return jax.nn.sigmoid(x)


def _dot(a, b):
    return jnp.dot(a, b, preferred_element_type=F32)


def _dot_nt(a, b):
    return lax.dot_general(a, b, (((1,), (1,)), ((), ())), preferred_element_type=F32)


def _dot_tn(a, b):
    return lax.dot_general(a, b, (((0,), (0,)), ((), ())), preferred_element_type=F32)


def _mod_kernel(c_ref, w_ref, b_ref, o_ref):
    c = c_ref[...]
    a = (c * _sigmoid(c)).astype(BF16)
    o_ref[...] = _dot(a, w_ref[...].astype(BF16)) + b_ref[...]


def _modulation(c_all, w_mod, b_mod):
    depth, d, n6 = w_mod.shape
    nb = c_all.shape[0]
    tn = 1536
    return pl.pallas_call(
        _mod_kernel,
        grid=(depth, n6 // tn),
        in_specs=[
            pl.BlockSpec((nb, d), lambda l, j: (0, 0)),
            pl.BlockSpec((None, d, tn), lambda l, j: (l, 0, j)),
            pl.BlockSpec((None, 1, tn), lambda l, j: (l, 0, j)),
        ],
        out_specs=pl.BlockSpec((None, nb, tn), lambda l, j: (l, 0, j)),
        out_shape=jax.ShapeDtypeStruct((depth, nb, n6), F32),
        compiler_params=_cparams(("parallel", "parallel")),
        name="modulation",
    )(c_all, w_mod, b_mod.reshape(depth, 1, n6))


def _swap_half(a):
    parts = []
    for j in range(a.shape[1] // LANE):
        s = a[:, LANE * j:LANE * (j + 1)]
        lane = lax.broadcasted_iota(jnp.int32, s.shape, 1)
        first = (lane & (A_DH - 1)) < (A_DH // 2)
        parts.append(jnp.where(first, pltpu.roll(s, LANE - A_DH // 2, 1), pltpu.roll(s, A_DH // 2, 1)))
    return jnp.concatenate(parts, axis=1)


def _in_kernel(x_ref, mod_ref, g_ref, w_ref, cos_ref, sin_ref, k_all_ref, v_all_ref,
               q_ref, kb_ref, vb_ref, k4_ref, v4_ref, b_ref, cu_ref, gt_ref):
    del k_all_ref, v_all_ref
    x = x_ref[...]
    ms = jnp.mean(x * x, axis=-1, keepdims=True)
    y = x * lax.rsqrt(ms + EPS) * g_ref[...]
    h = y * (1.0 + mod_ref[:, D_MODEL:2 * D_MODEL]) + mod_ref[:, 0:D_MODEL]
    hb = h.astype(BF16)

    def proj(c0, width):
        return _dot(hb, w_ref[:, c0:c0 + width])

    cos = cos_ref[...]
    sin = sin_ref[...]

    def rope(a):
        return a * cos + _swap_half(a) * sin

    q_ref[...] = (rope(proj(0, A_QK)) * (A_DH ** -0.5 * LOG2E)).astype(BF16)
    k = rope(proj(COL_AK, A_QK))
    v = proj(COL_AV, 512)
    kb_ref[...] = k.astype(BF16)
    vb_ref[...] = v.astype(BF16)
    tm = k.shape[0]
    for h in range(A_HEADS):
        k4_ref[pl.ds(h, tm, stride=A_HEADS), :] = k[:, h * LANE:(h + 1) * LANE]
        v4_ref[pl.ds(h, tm, stride=A_HEADS), :] = v[:, h * LANE:(h + 1) * LANE]
    for j in range(4):
        b_ref[:, 512 * j:512 * (j + 1)] = proj(COL_B + 512 * j, 512)
    cu_ref[...] = proj(COL_CU, C_WIDTH)
    for j in range(6):
        gt_ref[:, 512 * j:512 * (j + 1)] = proj(COL_G + 512 * j, 512)


def _in_projection(x, mod, g1, w_in, cos, sin, tm, layer, depth, k_all, v_all):
    bsz, t, d = x.shape
    nt = t // tm
    row = lambda b, i: (b, i, 0)
    cache_spec = pl.BlockSpec((None, None, tm * A_HEADS, LANE), lambda b, i: (layer, b, i, 0))
    cache_shape = jax.ShapeDtypeStruct((depth, bsz, t * A_HEADS, LANE), F32)
    in_specs = [
        pl.BlockSpec((None, tm, d), row),
        pl.BlockSpec((None, 1, 6 * d), lambda b, i: (b, 0, 0)),
        pl.BlockSpec((1, d), lambda b, i: (0, 0)),
        pl.BlockSpec((d, IN_COLS), lambda b, i: (0, 0), pipeline_mode=pl.Buffered(1)),
        pl.BlockSpec((tm, A_QK), lambda b, i: (i, 0)),
        pl.BlockSpec((tm, A_QK), lambda b, i: (i, 0)),
        pl.BlockSpec(memory_space=pl.ANY),
        pl.BlockSpec(memory_space=pl.ANY),
    ]
    args = [x, mod, g1, w_in, cos, sin, k_all, v_all]
    return pl.pallas_call(
        _in_kernel,
        grid=(bsz, nt),
        in_specs=in_specs,
        out_specs=[
            pl.BlockSpec((None, tm, A_QK), row),
            pl.BlockSpec((None, tm, A_QK), row),
            pl.BlockSpec((None, tm, 512), row),
            cache_spec,
            cache_spec,
            pl.BlockSpec((None, tm, 2048), row),
            pl.BlockSpec((tm, C_WIDTH), lambda b, i: (i, b)),
            pl.BlockSpec((None, tm, 3 * d), row),
        ],
        out_shape=[
            jax.ShapeDtypeStruct((bsz, t, A_QK), BF16),
            jax.ShapeDtypeStruct((bsz, t, A_QK), BF16),
            jax.ShapeDtypeStruct((bsz, t, 512), BF16),
            cache_shape,
            cache_shape,
            jax.ShapeDtypeStruct((bsz, t, 2048), F32),
            jax.ShapeDtypeStruct((t, bsz * C_WIDTH), F32),
            jax.ShapeDtypeStruct((bsz, t, 3 * d), F32),
        ],
        input_output_aliases={6: 3, 7: 4},
        compiler_params=_cparams(("parallel", "parallel")),
        name="in_projection",
    )(*args)


def _split_maps(q):
    lane = lax.broadcasted_iota(jnp.int32, q.shape, 1)
    zero = jnp.zeros_like(q)
    return jnp.concatenate([jnp.where(lane < A_DH, q, zero), jnp.where(lane >= A_DH, q, zero)], axis=0)


def _attn_finish(acc, l, lam, g, scale_out, tq):
    o = acc[:tq] / l[:tq] - lam * (acc[tq:] / l[tq:])
    ms = jnp.mean(o * o, axis=-1, keepdims=True)
    return (o * lax.rsqrt(ms + EPS) * g) * scale_out


def _attn_prompt_kernel(lam_ref, q_ref, k_ref, v_ref, g_ref, o_ref, vt, *, tq, scale_out):
    i = pl.program_id(1)
    nkv = vt.shape[0] // A_HEADS

    @pl.when(i == 0)
    def _():
        for h in range(A_HEADS):
            for jj in range(nkv):
                blk = (slice(jj * tq, (jj + 1) * tq), slice(h * LANE, (h + 1) * LANE))
                vt[h * nkv + jj] = jnp.concatenate(
                    [v_ref[blk].astype(F32).T.astype(BF16), jnp.ones((ONES_ROWS, tq), BF16)], axis=0)

    qq = [_split_maps(q_ref[:, h * LANE:(h + 1) * LANE]) for h in range(A_HEADS)]

    def scores(j, h):
        ks = k_ref[pl.ds(pl.multiple_of(j * tq, tq), tq), h * LANE:(h + 1) * LANE]
        return _dot_nt(ks, qq[h])

    def tile(j, carry, s, h, masked):
        m, acc = carry
        if masked:
            kr = lax.broadcasted_iota(jnp.int32, s.shape, 0)
            qc = lax.broadcasted_iota(jnp.int32, s.shape, 1)
            s = jnp.where((kr >> 6) <= ((qc & (tq - 1)) >> 6), s, MASK_NEG)
        mn = jnp.maximum(m, jnp.max(s, axis=0, keepdims=True))
        p = jnp.exp2((s - mn).astype(BF16))
        acc = jnp.exp2(m - mn) * acc + _dot(vt[h * nkv + j], p)
        return mn, acc

    def tiles(j, carries, masked):
        ss = [scores(j, h) for h in range(A_HEADS)]
        return tuple(tile(j, carries[h], ss[h], h, masked) for h in range(A_HEADS))

    init = (jnp.full((1, 2 * tq), MASK_NEG, F32), jnp.zeros((A_DV + ONES_ROWS, 2 * tq), F32))
    carries = lax.fori_loop(0, i, lambda j, c: tiles(j, c, False), (init,) * A_HEADS)
    carries = tiles(i, carries, True)
    for h in range(A_HEADS):
        acc = carries[h][1]
        acc = acc[:A_DV] / acc[A_DV:A_DV + 1]
        o = acc[:, :tq] - lam_ref[0:1, 0:1] * acc[:, tq:]
        ms = jnp.mean(o * o, axis=0, keepdims=True)
        y = (o * lax.rsqrt(ms + EPS) * g_ref[...]) * scale_out
        o_ref[:, h * LANE:(h + 1) * LANE] = y.T.astype(BF16)


def _attn_prompt(lam, q, k, v, g, scale_out, tq):
    bsz, t, _ = q.shape
    assert tq % CHUNK == 0 and tq & (tq - 1) == 0
    kern = functools.partial(_attn_prompt_kernel, tq=tq, scale_out=scale_out)
    return pl.pallas_call(
        kern,
        grid=(bsz, t // tq),
        in_specs=[
            pl.BlockSpec((1, LANE), lambda b, i: (0, 0)),
            pl.BlockSpec((None, tq, A_HEADS * LANE), lambda b, i: (b, i, 0)),
            pl.BlockSpec((None, t, A_HEADS * LANE), lambda b, i: (b, 0, 0)),
            pl.BlockSpec((None, t, A_HEADS * LANE), lambda b, i: (b, 0, 0)),
            pl.BlockSpec((A_DV, 1), lambda b, i: (0, 0)),
        ],
        out_specs=pl.BlockSpec((None, tq, A_HEADS * LANE), lambda b, i: (b, i, 0)),
        out_shape=jax.ShapeDtypeStruct((bsz, t, A_HEADS * A_DV), BF16),
        scratch_shapes=[pltpu.VMEM((A_HEADS * (t // tq), A_DV + ONES_ROWS, tq), BF16)],
        compiler_params=_cparams(("parallel", "arbitrary")),
        name="attn_prompt",
    )(lam, q, k, v, g.reshape(A_DV, 1))


def _attn_sample_kernel(lam_ref, q_ref, pk_ref, pv_ref, k_ref, v_ref, g_ref, o_ref, *, tq, scale_out):
    qq = _split_maps(q_ref[...])
    s_p = _dot_nt(qq, pk_ref[...].astype(BF16))
    s_n = _dot_nt(qq, k_ref[...])
    m = jnp.maximum(jnp.max(s_p, axis=-1, keepdims=True), jnp.max(s_n, axis=-1, keepdims=True))
    p_p = jnp.exp2(s_p - m)
    p_n = jnp.exp2(s_n - m)
    l = jnp.sum(p_p, axis=-1, keepdims=True) + jnp.sum(p_n, axis=-1, keepdims=True)
    acc = _dot(p_p.astype(BF16), pv_ref[...].astype(BF16)) + _dot(p_n.astype(BF16), v_ref[...])
    o_ref[...] = _attn_finish(acc, l, lam_ref[...], g_ref[...], scale_out, tq).astype(BF16)


def _attn_sample(lam, q, cache_k, cache_v, layer, k, v, g, scale_out):
    bsz, t, _ = q.shape
    past = cache_k.shape[2]
    kern = functools.partial(_attn_sample_kernel, tq=t, scale_out=scale_out)
    new = lambda b, h: (b, 0, h)
    old = lambda b, h: (layer, b, 0, h)
    return pl.pallas_call(
        kern,
        grid=(bsz, A_HEADS),
        in_specs=[
            pl.BlockSpec((1, LANE), lambda b, h: (0, 0)),
            pl.BlockSpec((None, t, LANE), new),
            pl.BlockSpec((None, None, past, LANE), old),
            pl.BlockSpec((None, None, past, LANE), old),
            pl.BlockSpec((None, t, LANE), new),
            pl.BlockSpec((None, t, LANE), new),
            pl.BlockSpec((1, LANE), lambda b, h: (0, 0)),
        ],
        out_specs=pl.BlockSpec((None, t, LANE), new),
        out_shape=jax.ShapeDtypeStruct((bsz, t, A_HEADS * A_DV), BF16),
        compiler_params=_cparams(("parallel", "parallel")),
        name="attn_sample",
    )(lam, q, cache_k, cache_v, k, v, g)


def _split3(x):
    hi = x.astype(BF16)
    r = x - hi.astype(F32)
    mid = r.astype(BF16)
    lo = (r - mid.astype(F32)).astype(BF16)
    return hi, mid, lo


HGRN_LEVELS = (5, 4, 3)
HGRN_DIAG = 8


def _hgrn_tables():
    L = CHUNK
    r = np.arange(L)[:, None]
    c = np.arange(L)[None, :]
    sgn, up, pair = [], [], []
    for sh in HGRN_LEVELS:
        upper = ((r >> sh) & 1) == 1
        up.append(np.broadcast_to(upper, (L, B_DK)))
        sgn.append(np.broadcast_to(np.where(upper, 1.0, -1.0), (L, B_DK)))
        pair.append((((r >> sh) & 1) == 1) & ((r >> (sh + 1)) == (c >> (sh + 1))) & (((c >> sh) & 1) == 0))
    diag = [(c == r - d) & ((r & (HGRN_DIAG - 1)) >= d) for d in range(HGRN_DIAG)]
    tril = (c <= r).astype(np.float32)
    rowm = jnp.asarray(np.stack(sgn + up).astype(np.float32))
    sqm = jnp.asarray(np.stack(pair + diag).astype(np.float32))
    tril3 = jnp.asarray(np.concatenate([tril, tril, tril], axis=1), dtype=BF16)
    return tril3, rowm, sqm


def _hgrn_chunk(heads, gn, tril3, rowm_ref, sqm_ref):
    L = CHUNK
    nl = len(HGRN_LEVELS)
    n = len(heads)

    fs, kfs, qs, bs = [], [], [], []
    for qraw, fraw, v, graw, lb, st in heads:
        f = lb + (1.0 - lb) * _sigmoid_rel(fraw)
        hi, mid, lo = _split3(jnp.log(f))
        fs.append(f)
        kfs.append(1.0 - f)
        qs.append(qraw * _sigmoid(qraw))
        bs.append(_dot(tril3, jnp.concatenate([hi, mid, lo], axis=0)))

    os_, atts, sts, vbs = [], [], [], []
    for i in range(n):
        qraw, fraw, v, graw, lb, st = heads[i]
        q, kf, b = qs[i], kfs[i], bs[i]
        vb = v.astype(BF16)
        o = _dot_nt((q * jnp.exp(b)).astype(BF16), st.astype(BF16))
        att = jnp.zeros((L, L), F32)
        for li, sh in enumerate(HGRN_LEVELS):
            m = 1 << sh
            ref_rows = [jnp.broadcast_to(b[(2 * j + 1) * m - 1:(2 * j + 1) * m, :], (2 * m, B_DK))
                        for j in range(L // (2 * m))]
            ref = ref_rows[0] if len(ref_rows) == 1 else jnp.concatenate(ref_rows, axis=0)
            w = jnp.exp((b - ref) * rowm_ref[li])
            wq = w * rowm_ref[nl + li]
            att = att + sqm_ref[li] * _dot_nt((q * wq).astype(BF16), (kf * (w - wq)).astype(BF16))
        b_last = b[L - 1:L, :]
        kdec = (kf * jnp.exp(b_last - b)).astype(BF16)
        sts.append(st * jnp.exp(b_last) + _dot_tn(vb, kdec))
        os_.append(o)
        atts.append(att)
        vbs.append(vb)

    outs = []
    for i in range(n):
        q, f, att = qs[i], fs[i], atts[i]
        hd = kfs[i]
        for d in range(HGRN_DIAG):
            if d > 0:
                hd = f * pltpu.roll(hd, 1, 0)
            att = att + sqm_ref[nl + d] * jnp.sum(q * hd, axis=-1, keepdims=True)
        outs.append(os_[i] + _dot(att.astype(BF16), vbs[i]))

    res = []
    for i in range(n):
        o, graw = outs[i], heads[i][3]
        ms = jnp.mean(o * o, axis=-1, keepdims=True)
        res.append((o * lax.rsqrt(ms + EPS) * gn * (graw * _sigmoid(graw)), sts[i]))
    return res


def _hgrn_kernel(*refs, nchunk, has_s0):
    if has_s0:
        q_ref, f_ref, v_ref, g_ref, lb_ref, gn_ref, tril_ref, rowm_ref, sqm_ref, s0_ref, y_ref, sf_ref, st = refs
    else:
        q_ref, f_ref, v_ref, g_ref, lb_ref, gn_ref, tril_ref, rowm_ref, sqm_ref, y_ref, sf_ref, st = refs
    c = pl.program_id(1)

    @pl.when(c == 0)
    def _():
        for h in range(B_HEADS):
            st[h] = s0_ref[h].T if has_s0 else jnp.zeros((LANE, B_DK), F32)

    gn = gn_ref[...]
    tril3 = tril_ref[...]

    def body(n, carry):
        sl = pl.ds(pl.multiple_of(n * CHUNK, CHUNK), CHUNK)
        hsl = [slice(h * LANE, (h + 1) * LANE) for h in range(B_HEADS)]
        heads = [(q_ref[sl, hs], f_ref[sl, hs], v_ref[sl, hs], g_ref[sl, hs], lb_ref[:, hs], st[h])
                 for h, hs in enumerate(hsl)]
        for h, (y, st_new) in enumerate(_hgrn_chunk(heads, gn, tril3, rowm_ref, sqm_ref)):
            st[h] = st_new
            y_ref[sl, hsl[h]] = y.astype(BF16)
        return carry

    lax.fori_loop(0, nchunk, body, 0, unroll=min(2, nchunk))

    @pl.when(c == pl.num_programs(1) - 1)
    def _():
        for h in range(B_HEADS):
            sf_ref[h] = st[h].T


def _hgrn(braw, lb, gn, s0, layer, tb):
    bsz, t, _ = braw.shape
    has_s0 = s0 is not None
    kern = functools.partial(_hgrn_kernel, nchunk=tb // CHUNK, has_s0=has_s0)
    width = B_HEADS * LANE
    tril3, rowm, sqm = _hgrn_tables()

    def col(j):
        return pl.BlockSpec((None, tb, width), lambda b, c: (b, c, j))

    def const(shape):
        return pl.BlockSpec(shape, lambda b, c: (0,) * len(shape))

    in_specs = [col(0), col(1), col(2), col(3), const((1, width)), const((1, LANE)),
                const(tril3.shape), const(rowm.shape), const(sqm.shape)]
    args = [braw, braw, braw, braw, lb, gn, tril3, rowm, sqm]
    if has_s0:
        in_specs.append(pl.BlockSpec((None, None, B_HEADS, B_DK, LANE), lambda b, c: (layer, b, 0, 0, 0)))
        args.append(s0)
    return pl.pallas_call(
        kern,
        grid=(bsz, t // tb),
        in_specs=in_specs,
        out_specs=[
            pl.BlockSpec((None, tb, width), lambda b, c: (b, c, 0)),
            pl.BlockSpec((None, B_HEADS, B_DK, LANE), lambda b, c: (b, 0, 0, 0)),
        ],
        out_shape=[
            jax.ShapeDtypeStruct((bsz, t, width), BF16),
            jax.ShapeDtypeStruct((bsz, B_HEADS, B_DK, LANE), F32),
        ],
        scratch_shapes=[pltpu.VMEM((B_HEADS, LANE, B_DK), F32)],
        compiler_params=_cparams(("parallel", "arbitrary")),
        name="hgrn",
    )(*args)


S5_COLS = 512
S5_BLOCK = 256


def _s5_kernel(u_ref, bm_ref, cm_ref, a_ref, d_ref, h0_ref, y_ref, hf_ref, xs, hs, *, tc, nb):
    step_id = pl.program_id(0)

    @pl.when(step_id == 0)
    def _():
        hs[...] = h0_ref[...]

    u = u_ref[...]
    ub = u.astype(BF16)
    for j in range(C_LANES // S5_BLOCK):
        q = (j * S5_BLOCK // C_STATE * C_GROUP) // LANE
        rows_k = slice(LANE * q, LANE * (q + 1))
        for part in (0, C_LANES):
            cols = slice(part + S5_BLOCK * j, part + S5_BLOCK * (j + 1))
            xs[:, cols] = _dot(ub[:, rows_k], bm_ref[rows_k, cols])
    for cc in range(C_LANES // S5_COLS):
        re = slice(cc * S5_COLS, (cc + 1) * S5_COLS)
        im = slice(C_LANES + cc * S5_COLS, C_LANES + (cc + 1) * S5_COLS)
        ar = a_ref[0:nb, re]
        ai = a_ref[nb:2 * nb, re]

        def step(t, carry):
            hr, hi = carry
            rows = pl.ds(pl.multiple_of(t * nb, nb), nb)
            nr = ar * hr - ai * hi + xs[rows, re]
            ni = ar * hi + ai * hr + xs[rows, im]
            xs[rows, re] = nr
            xs[rows, im] = ni
            return nr, ni

        hr, hi = lax.fori_loop(0, tc, step, (hs[:, re], hs[:, im]), unroll=8)
        hs[:, re] = hr
        hs[:, im] = hi
    for c in range(C_WIDTH // S5_BLOCK):
        ch = slice(S5_BLOCK * c, S5_BLOCK * (c + 1))
        n_st = S5_BLOCK // C_GROUP * C_STATE
        y = d_ref[:, ch] * u[:, ch]
        for part in (0, C_LANES):
            st = slice(part + n_st * c, part + n_st * (c + 1))
            y = y + _dot(xs[:, st].astype(BF16), cm_ref[st, ch])
        y_ref[:, ch] = y.astype(BF16)
    hf_ref[...] = hs[...]


def _s5(u_tm, bm, cm, a, d, h0, tc):
    n, _ = u_tm.shape
    nb = h0.shape[0]
    rows = tc * nb
    kern = functools.partial(_s5_kernel, tc=tc, nb=nb)
    const = lambda i: (0, 0)
    return pl.pallas_call(
        kern,
        grid=(n // rows,),
        in_specs=[
            pl.BlockSpec((rows, C_WIDTH), lambda i: (i, 0)),
            pl.BlockSpec((C_WIDTH, 2 * C_LANES), const, pipeline_mode=pl.Buffered(1)),
            pl.BlockSpec((2 * C_LANES, C_WIDTH), const, pipeline_mode=pl.Buffered(1)),
            pl.BlockSpec((2 * nb, C_LANES), const),
            pl.BlockSpec((1, C_WIDTH), const),
            pl.BlockSpec((nb, 2 * C_LANES), const),
        ],
        out_specs=[
            pl.BlockSpec((rows, C_WIDTH), lambda i: (i, 0)),
            pl.BlockSpec((nb, 2 * C_LANES), const),
        ],
        out_shape=[
            jax.ShapeDtypeStruct((n, C_WIDTH), BF16),
            jax.ShapeDtypeStruct((nb, 2 * C_LANES), F32),
        ],
        scratch_shapes=[pltpu.VMEM((rows, 2 * C_LANES), F32), pltpu.VMEM((nb, 2 * C_LANES), F32)],
        compiler_params=_cparams(("arbitrary",)),
        name="s5",
    )(u_tm, bm, cm, a, d, h0)


def _s5_tables(a_re, a_im, b_re, b_im, c_re, c_im, log_dt, nb):
    dt = jnp.exp(log_dt)[:, None]
    mag = jnp.exp(a_re * dt)
    abr, abi = mag * jnp.cos(a_im * dt), mag * jnp.sin(a_im * dt)
    den = a_re * a_re + a_im * a_im
    coef_r = ((abr - 1.0) * a_re + abi * a_im) / den
    coef_i = (abi * a_re - (abr - 1.0) * a_im) / den
    fr = coef_r[..., None] * b_re - coef_i[..., None] * b_im
    fi = coef_r[..., None] * b_im + coef_i[..., None] * b_re
    eye = jnp.eye(C_NGROUPS, dtype=F32)

    def blockdiag_in(m):
        return jnp.einsum('gpj,gh->gjhp', m, eye).reshape(C_WIDTH, C_LANES)

    def blockdiag_out(m):
        return jnp.einsum('gjp,gh->gphj', m, eye).reshape(C_LANES, C_WIDTH)

    bm = jnp.concatenate([blockdiag_in(fr), blockdiag_in(fi)], axis=1).astype(BF16)
    cm = jnp.concatenate([blockdiag_out(c_re), -blockdiag_out(c_im)], axis=0).astype(BF16)
    a = jnp.concatenate([jnp.broadcast_to(abr.reshape(1, C_LANES), (nb, C_LANES)),
                         jnp.broadcast_to(abi.reshape(1, C_LANES), (nb, C_LANES))], axis=0)
    return bm, cm, a


def _pair_max_sum(v):
    best = v[0:1] + v[1:2]
    for i, j in ((0, 2), (0, 3), (1, 2), (1, 3), (2, 3)):
        best = jnp.maximum(best, v[i:i + 1] + v[j:j + 1])
    return best


def _route(logits_t, rb):
    mx = jnp.max(logits_t, axis=0, keepdims=True)
    ex = jnp.exp(logits_t - mx)
    s = ex / jnp.sum(ex, axis=0, keepdims=True)
    sb = s + rb
    g = EXPERTS_PER_GROUP
    best = _pair_max_sum(sb[0:g])
    sel = jnp.zeros(best.shape, jnp.int32)
    for gi in range(1, N_EXPERT_GROUPS):
        sc = _pair_max_sum(sb[gi * g:(gi + 1) * g])
        take = sc > best
        best = jnp.where(take, sc, best)
        sel = jnp.where(take, gi, sel)
    sb4 = jnp.zeros((g,) + best.shape[1:], F32)
    s4 = jnp.zeros((g,) + best.shape[1:], F32)
    for gi in range(N_EXPERT_GROUPS):
        on = sel == gi
        sb4 = jnp.where(on, sb[gi * g:(gi + 1) * g], sb4)
        s4 = jnp.where(on, s[gi * g:(gi + 1) * g], s4)
    rows, keeps = [], []
    for e in range(g):
        rank = jnp.zeros(best.shape, jnp.int32)
        for j in range(g):
            if j == e:
                continue
            ahead = (sb4[j:j + 1] > sb4[e:e + 1]) if j > e else (sb4[j:j + 1] >= sb4[e:e + 1])
            rank = rank + ahead.astype(jnp.int32)
        rows.append(jnp.where(rank < 2, s4[e:e + 1], 0.0))
        keeps.append(jnp.where(rank < 2, 1.0, 0.0))
    w4 = jnp.concatenate(rows, axis=0)
    w4 = w4 / jnp.sum(w4, axis=0, keepdims=True)
    keep4 = jnp.concatenate(keeps, axis=0)
    gates = jnp.concatenate([jnp.where(sel == gi, w4, 0.0) for gi in range(N_EXPERT_GROUPS)], axis=0)
    chosen = jnp.concatenate([jnp.where(sel == gi, keep4, 0.0) for gi in range(N_EXPERT_GROUPS)], axis=0) > 0.5
    eid = lax.broadcasted_iota(jnp.int32, gates.shape, 0).astype(F32)
    e_lo = jnp.min(jnp.where(chosen, eid, float(N_EXPERTS)), axis=0, keepdims=True)
    e_hi = jnp.max(jnp.where(chosen, eid, -1.0), axis=0, keepdims=True)
    w_lo = jnp.sum(jnp.where(eid == e_lo, gates, 0.0), axis=0, keepdims=True)
    w_hi = jnp.sum(jnp.where(eid == e_hi, gates, 0.0), axis=0, keepdims=True)
    n = gates.shape[1]
    tri = jnp.where(lax.broadcasted_iota(jnp.int32, (n, n), 0) <= lax.broadcasted_iota(jnp.int32, (n, n), 1),
                    1.0, 0.0).astype(BF16)
    chosen_f = jnp.where(chosen, 1.0, 0.0)
    cum = _dot(chosen_f.astype(BF16), tri)
    before = cum - chosen_f
    r_lo = jnp.sum(jnp.where(eid == e_lo, before, 0.0), axis=0, keepdims=True)
    r_hi = jnp.sum(jnp.where(eid == e_hi, before, 0.0), axis=0, keepdims=True)
    return gates, jnp.concatenate([e_lo, e_hi, w_lo, w_hi, r_lo, r_hi], axis=0), cum[:, n - 1:n]


def _merge_kernel(x_ref, ya_ref, yb_ref, ys_ref, gt_ref, mod_ref, g2_ref,
                  wglu_ref, wa_ref, wb_ref, wc_ref, wout_ref, rw_ref, rb_ref,
                  x1_ref, h2_ref, gates_ref, counts_ref):
    d = D_MODEL
    z = _dot(ys_ref[...], wglu_ref[...])
    yc = (z[:, :C_WIDTH] * _sigmoid(z[:, C_WIDTH:])).astype(BF16)
    merged = (_sigmoid(gt_ref[:, 0:d]) * _dot(ya_ref[...], wa_ref[...])
              + _sigmoid(gt_ref[:, d:2 * d]) * _dot(yb_ref[...], wb_ref[...])
              + _sigmoid(gt_ref[:, 2 * d:3 * d]) * _dot(yc, wc_ref[...]))
    mix = _dot(merged.astype(BF16), wout_ref[...])
    x1 = x_ref[...] + mod_ref[:, 2 * d:3 * d] * mix
    x1_ref[...] = x1
    ms = jnp.mean(x1 * x1, axis=-1, keepdims=True)
    h2 = (x1 * lax.rsqrt(ms + EPS) * g2_ref[...]) * (1.0 + mod_ref[:, 4 * d:5 * d]) + mod_ref[:, 3 * d:4 * d]
    h_hi, h_mid, h_lo = _split3(h2)
    h2_ref[...] = h_hi.astype(h2_ref.dtype)
    hcat = jnp.concatenate([h_hi, h_mid, h_hi, h_lo, h_mid, h_hi], axis=1)
    gates_t, picks_t, counts = _route(_dot_nt(rw_ref[...], hcat), rb_ref[...])
    pad = jnp.zeros((LANE - N_EXPERTS - picks_t.shape[0], gates_t.shape[1]), F32)
    gates_ref[...] = jnp.concatenate([gates_t, picks_t, pad], axis=0).T
    counts_ref[...] = jnp.broadcast_to(counts, counts_ref.shape)


def _merge(x, ya, yb, ys_tm, graw, mod, g2, wglu, wa, wb, wc, wout, rw_t, rb, tm, h2_dtype):
    bsz, t, d = x.shape
    row = lambda b, i: (b, i, 0)
    const = lambda b, i: (0, 0)

    def wspec(shape):
        return pl.BlockSpec(shape, const, pipeline_mode=pl.Buffered(1))

    return pl.pallas_call(
        _merge_kernel,
        grid=(bsz, t // tm),
        in_specs=[
            pl.BlockSpec((None, tm, d), row),
            pl.BlockSpec((None, tm, 512), row),
            pl.BlockSpec((None, tm, 512), row),
            pl.BlockSpec((tm, C_WIDTH), lambda b, i: (i, b)),
            pl.BlockSpec((None, tm, 3 * d), row),
            pl.BlockSpec((None, 1, 6 * d), lambda b, i: (b, 0, 0)),
            pl.BlockSpec((1, d), const),
            wspec((C_WIDTH, 2 * C_WIDTH)), wspec((512, d)), wspec((512, d)), wspec((C_WIDTH, d)),
            wspec((d, d)), wspec((N_EXPERTS, 6 * d)), wspec((N_EXPERTS, 1)),
        ],
        out_specs=[
            pl.BlockSpec((None, tm, d), row),
            pl.BlockSpec((None, tm, d), row),
            pl.BlockSpec((None, tm, LANE), row),
            pl.BlockSpec((None, None, N_EXPERTS, LANE), lambda b, i: (b, i, 0, 0)),
        ],
        out_shape=[
            jax.ShapeDtypeStruct((bsz, t, d), F32),
            jax.ShapeDtypeStruct((bsz, t, d), h2_dtype),
            jax.ShapeDtypeStruct((bsz, t, LANE), F32),
            jax.ShapeDtypeStruct((bsz, t // tm, N_EXPERTS, LANE), F32),
        ],
        compiler_params=_cparams(("parallel", "parallel")),
        name="merge_router",
    )(x, ya, yb, ys_tm, graw, mod, g2, wglu, wa, wb, wc, wout, rw_t, rb)


def _moe_kernel(h_ref, gates_ref, x1_ref, mod_ref, wg_ref, wu_ref, wd_ref, gf_ref, o_ref, acc, *, final):
    e = pl.program_id(1)
    nbt, tb, d = h_ref.shape
    tm = nbt * tb

    @pl.when(e == 0)
    def _():
        acc[...] = jnp.zeros_like(acc)

    h = h_ref[...].reshape(tm, d)
    a = _dot(h, wg_ref[...].astype(BF16))
    u = _dot(h, wu_ref[...].astype(BF16))
    gates = gates_ref[...].reshape(tm, LANE)
    lane = lax.broadcasted_iota(jnp.int32, gates.shape, 1)
    gcol = jnp.sum(jnp.where(lane == e, gates, 0.0), axis=-1, keepdims=True)
    hid = (a * _sigmoid(a)) * u * gcol
    acc[...] += _dot(hid.astype(BF16), wd_ref[...].astype(BF16))

    @pl.when(e == pl.num_programs(1) - 1)
    def _():
        x2 = x1_ref[...] + mod_ref[:, :, 5 * d:6 * d] * acc[...].reshape(nbt, tb, d)
        if final:
            ms = jnp.mean(x2 * x2, axis=-1, keepdims=True)
            x2 = x2 * lax.rsqrt(ms + EPS) * gf_ref[...]
        o_ref[...] = x2


def _moe(h2, gates, x1, mod, wg, wu, wd, gfinal, layer, nbt, tb, final):
    bsz, t, d = x1.shape
    kern = functools.partial(_moe_kernel, final=final)
    tok = lambda i, e: (i // (t // tb), i % (t // tb), 0) if nbt == 1 else (i, 0, 0)
    nsteps = (bsz // nbt) * (t // tb)
    return pl.pallas_call(
        kern,
        grid=(nsteps, N_EXPERTS),
        in_specs=[
            pl.BlockSpec((nbt, tb, d), tok),
            pl.BlockSpec((nbt, tb, LANE), tok),
            pl.BlockSpec((nbt, tb, d), tok),
            pl.BlockSpec((nbt, 1, 6 * d), lambda i, e: ((i // (t // tb)) if nbt == 1 else i, 0, 0)),
            pl.BlockSpec((None, None, d, EXPERT_DFF), lambda i, e: (layer, e, 0, 0)),
            pl.BlockSpec((None, None, d, EXPERT_DFF), lambda i, e: (layer, e, 0, 0)),
            pl.BlockSpec((None, None, EXPERT_DFF, d), lambda i, e: (layer, e, 0, 0)),
            pl.BlockSpec((1, d), lambda i, e: (0, 0)),
        ],
        out_specs=pl.BlockSpec((nbt, tb, d), tok),
        out_shape=jax.ShapeDtypeStruct((bsz, t, d), F32),
        scratch_shapes=[pltpu.VMEM((nbt * tb, d), F32)],
        compiler_params=_cparams(("parallel", "arbitrary")),
        name="moe",
    )(h2, gates, x1, mod, wg, wu, wd, gfinal)


def _moe_plan(info, counts, tok_tile, tm):
    n = info.shape[0]
    n_tiles = (2 * n) // tm + N_EXPERTS
    counts = counts.astype(jnp.int32)
    total = jnp.sum(counts, axis=0)
    padded = ((total + tm - 1) // tm) * tm
    ends = jnp.cumsum(padded)
    offs = ends - padded
    tile_base = offs[None, :] + jnp.cumsum(counts, axis=0) - counts
    base_tok = jnp.repeat(tile_base, tok_tile, axis=0)
    experts = jnp.arange(N_EXPERTS, dtype=jnp.int32)[None, :]
    slots = []
    for k in range(2):
        e = info[:, N_EXPERTS + k].astype(jnp.int32)
        r = info[:, N_EXPERTS + 4 + k].astype(jnp.int32)
        slots.append(jnp.sum(jnp.where(e[:, None] == experts, base_tok, 0), axis=1) + r)
    starts = jnp.arange(n_tiles, dtype=jnp.int32) * tm
    tile_expert = jnp.minimum(jnp.sum((starts[:, None] >= ends[None, :]).astype(jnp.int32), axis=1), N_EXPERTS - 1)
    fill = jnp.stack([offs + total, ends]).astype(jnp.int32)
    return jnp.stack(slots, axis=1), tile_expert.astype(jnp.int32), (ends[-1:] // tm).astype(jnp.int32), fill


def _row_wait(count, src, dst, sem):
    def body(r, carry):
        pltpu.make_async_copy(src.at[pl.ds(0, 1)], dst.at[pl.ds(0, 1)], sem).wait()
        return carry
    lax.fori_loop(0, count, body, 0, unroll=8)


def _moe_dispatch_kernel(fill_ref, sl_ref, h_ref, xs_hbm, zrow, sems, *, n_rows, tm):
    i = pl.program_id(0)
    sem = sems.at[0]
    td = h_ref.shape[0]
    n_pad = N_EXPERTS * tm

    @pl.when(i == 0)
    def _():
        zrow[...] = jnp.zeros_like(zrow)

        def zero_rows(lo, hi):
            def body(s, carry):
                pltpu.make_async_copy(zrow.at[pl.ds(0, 1)], xs_hbm.at[pl.ds(s, 1)], sem).start()
                return carry
            lax.fori_loop(lo, hi, body, 0)

        for e in range(N_EXPERTS):
            zero_rows(fill_ref[0, e], fill_ref[1, e])
        zero_rows(fill_ref[1, N_EXPERTS - 1], n_rows)
        _row_wait(n_pad, zrow, xs_hbm, sem)

    for r in range(td):
        for k in range(2):
            pltpu.make_async_copy(h_ref.at[pl.ds(r, 1)], xs_hbm.at[pl.ds(sl_ref[0, k * td + r], 1)],
                                  sem).start(priority=k)
    _row_wait(2 * td, h_ref, xs_hbm, sem)


def _moe_dispatch(h2, slot, fill, n_rows, td, tm):
    n, d = h2.shape
    nt = n // td
    slots = slot.reshape(nt, td, 2).transpose(0, 2, 1).reshape(nt, 1, 2 * td)
    grid_spec = pltpu.PrefetchScalarGridSpec(
        num_scalar_prefetch=1,
        grid=(nt,),
        in_specs=[
            pl.BlockSpec((None, 1, 2 * td), lambda i, fl: (i, 0, 0), memory_space=pltpu.SMEM),
            pl.BlockSpec((td, d), lambda i, fl: (i, 0)),
        ],
        out_specs=pl.BlockSpec(memory_space=pl.ANY),
        scratch_shapes=[pltpu.VMEM((8, d), F32), pltpu.SemaphoreType.DMA((1,))],
    )
    return pl.pallas_call(
        functools.partial(_moe_dispatch_kernel, n_rows=n_rows, tm=tm),
        grid_spec=grid_spec,
        out_shape=jax.ShapeDtypeStruct((n_rows, d), F32),
        compiler_params=_cparams(("arbitrary",)),
        name="moe_dispatch",
    )(fill, slots, h2), slots


def _moe_gemm_kernel(te_ref, nv_ref, x_ref, wg_ref, wu_ref, wd_ref, y_ref):
    i = pl.program_id(0)
    nv = nv_ref[0]

    @pl.when(i < nv)
    def _():
        x = x_ref[...].astype(BF16)
        a = _dot(x, wg_ref[...].astype(BF16))
        u = _dot(x, wu_ref[...].astype(BF16))
        y_ref[...] = _dot(((a * _sigmoid(a)) * u).astype(BF16), wd_ref[...].astype(BF16))

    @pl.when(i >= nv)
    def _():
        y_ref[...] = jnp.zeros_like(y_ref)


def _moe_gemm(xs, tile_expert, n_valid, wg, wu, wd, layer, tm):
    n_rows, d = xs.shape
    n_tiles = n_rows // tm
    grid_spec = pltpu.PrefetchScalarGridSpec(
        num_scalar_prefetch=2,
        grid=(n_tiles,),
        in_specs=[
            pl.BlockSpec((tm, d), lambda i, te, nv: (i, 0)),
            pl.BlockSpec((None, None, d, EXPERT_DFF), lambda i, te, nv: (layer, te[i], 0, 0)),
            pl.BlockSpec((None, None, d, EXPERT_DFF), lambda i, te, nv: (layer, te[i], 0, 0)),
            pl.BlockSpec((None, None, EXPERT_DFF, d), lambda i, te, nv: (layer, te[i], 0, 0)),
        ],
        out_specs=pl.BlockSpec((tm, d), lambda i, te, nv: (i, 0)),
    )
    return pl.pallas_call(
        _moe_gemm_kernel,
        grid_spec=grid_spec,
        out_shape=jax.ShapeDtypeStruct((n_rows, d), F32),
        compiler_params=_cparams(("arbitrary",)),
        name="moe_gemm",
    )(tile_expert, n_valid, xs, wg, wu, wd)


def _moe_combine_kernel(sl_cur, sl_nxt, y_hbm, info_ref, x1_ref, mod_ref, gf_ref, o_ref, ybuf, sems, *, final):
    i = pl.program_id(0)
    n = pl.num_programs(0)
    tc = info_ref.shape[0]
    d = D_MODEL
    cur = lax.rem(i, 2)

    def gather(sl_ref, buf):
        for q in range(2 * tc):
            pltpu.make_async_copy(y_hbm.at[pl.ds(sl_ref[0, q], 1)], ybuf.at[buf, pl.ds(q, 1)],
                                  sems.at[buf]).start(priority=q % 2)

    @pl.when(i == 0)
    def _():
        gather(sl_cur, 0)

    for buf in range(2):
        @pl.when((i + 1 < n) & (cur == 1 - buf))
        def _():
            gather(sl_nxt, buf)

    _row_wait(2 * tc, y_hbm, ybuf.at[cur], sems.at[cur])
    w_lo = info_ref[:, N_EXPERTS + 2:N_EXPERTS + 3]
    w_hi = info_ref[:, N_EXPERTS + 3:N_EXPERTS + 4]
    moe = w_lo * ybuf[cur, 0:tc, :] + w_hi * ybuf[cur, tc:2 * tc, :]
    x2 = x1_ref[...] + mod_ref[:, 5 * d:6 * d] * moe
    if final:
        ms = jnp.mean(x2 * x2, axis=-1, keepdims=True)
        x2 = x2 * lax.rsqrt(ms + EPS) * gf_ref[...]
    o_ref[...] = x2


def _moe_combine(y, slots, info, x1, mod, gfinal, final):
    bsz, t, d = x1.shape
    n = bsz * t
    nt = slots.shape[0]
    tc = slots.shape[2] // 2
    per_b = t // tc
    row = lambda i: (i, 0)
    out = pl.pallas_call(
        functools.partial(_moe_combine_kernel, final=final),
        grid=(nt,),
        in_specs=[
            pl.BlockSpec((None, 1, 2 * tc), lambda i: (i, 0, 0), memory_space=pltpu.SMEM),
            pl.BlockSpec((None, 1, 2 * tc), lambda i: (jnp.minimum(i + 1, nt - 1), 0, 0), memory_space=pltpu.SMEM),
            pl.BlockSpec(memory_space=pl.ANY),
            pl.BlockSpec((tc, LANE), row),
            pl.BlockSpec((tc, d), row),
            pl.BlockSpec((None, 1, 6 * d), lambda i: (i // per_b, 0, 0)),
            pl.BlockSpec((1, d), lambda i: (0, 0)),
        ],
        out_specs=pl.BlockSpec((tc, d), row),
        out_shape=jax.ShapeDtypeStruct((n, d), F32),
        scratch_shapes=[pltpu.VMEM((2, 2 * tc, d), F32), pltpu.SemaphoreType.DMA((2,))],
        compiler_params=_cparams(("arbitrary",)),
        name="moe_combine",
    )(slots, slots, y, info.reshape(n, LANE), x1.reshape(n, d), mod, gfinal)
    return out.reshape(bsz, t, d)


def _rope_tables(pos):
    inv = ROPE_THETA ** (-jnp.arange(0, A_DH, 2, dtype=F32) / A_DH)
    ang = pos.astype(F32)[:, None] * inv[None, :]
    cos, sin = jnp.cos(ang), jnp.sin(ang)
    seg_cos = jnp.concatenate([cos, cos], axis=1)
    seg_sin = jnp.concatenate([-sin, sin], axis=1)
    reps = A_QK // A_DH
    return jnp.tile(seg_cos, (1, reps)), jnp.tile(seg_sin, (1, reps))


def _trunk(x, mods, pos, p, wts, past, tiles):
    bsz, t, d = x.shape
    depth = mods.shape[0]
    cos, sin = _rope_tables(pos)
    lb_p = jax.nn.softmax(p['hgrn_lb'].astype(F32), axis=0)
    lbs = jnp.cumsum(lb_p, axis=0) - lb_p[0:1]
    w_hi, w_mid, w_lo = _split3(p['router_w'].T.astype(F32))
    rw_t = jnp.concatenate([w_hi, w_hi, w_mid, w_hi, w_mid, w_lo], axis=1)
    rb = p['router_b'].reshape(N_EXPERTS, 1).astype(F32)
    gfinal = p['final_norm_g'].reshape(1, d)
    hs, rs, ims = [], [], []
    k_all = jnp.zeros((depth, bsz, t * A_HEADS, LANE), F32)
    v_all = jnp.zeros((depth, bsz, t * A_HEADS, LANE), F32)
    for l in range(depth):
        mod = mods[l]
        q, k, v, k_all, v_all, braw, cu_tm, graw = _in_projection(
            x, mod, p['norm1_g'][l].reshape(1, d), wts['w_in'][l], cos, sin, tiles['tm'], l, depth, k_all, v_all)

        lam_init = 0.8 - 0.6 * math.exp(-0.3 * l)
        lp = p['diff_lambda'][l].astype(F32)
        lam = jnp.exp(jnp.sum(lp[0] * lp[1])) - jnp.exp(jnp.sum(lp[2] * lp[3])) + lam_init
        lam_row = jnp.full((1, LANE), lam, F32)
        subg = p['diff_subln_g'][l].reshape(1, A_DV)
        if past is None:
            ya = _attn_prompt(lam_row, q, k, v, subg, 1.0 - lam_init, tiles['tq'])
        else:
            pk = past[0].reshape(depth, bsz, -1, A_HEADS * 2 * A_DH)
            pv = past[1].reshape(depth, bsz, -1, A_HEADS * A_DV)
            ya = _attn_sample(lam_row, q, pk, pv, l, k, v, subg, 1.0 - lam_init)

        yb, s_h = _hgrn(braw, lbs[l].reshape(1, B_HEADS * B_DK), p['hgrn_norm_g'][l].reshape(1, LANE),
                        None if past is None else past[2], l, tiles['tb'])

        bm, cm, a_tab = _s5_tables(p['s5_a_re'][l], p['s5_a_im'][l], p['s5_b_re'][l], p['s5_b_im'][l],
                                   p['s5_c_re'][l], p['s5_c_im'][l], p['s5_log_dt'][l], bsz)
        if past is None:
            h0 = jnp.zeros((bsz, 2 * C_LANES), F32)
        else:
            h0 = jnp.concatenate([past[3][l].reshape(bsz, C_LANES), past[4][l].reshape(bsz, C_LANES)], axis=1)
        ys_tm, s5_state = _s5(cu_tm.reshape(t * bsz, C_WIDTH), bm, cm, a_tab,
                              p['s5_d'][l].reshape(1, C_WIDTH), h0, tiles['tc'])

        x1, h2, gates, counts = _merge(x, ya, yb, ys_tm.reshape(t, bsz * C_WIDTH), graw, mod,
                               p['norm2_g'][l].reshape(1, d), wts['w_glu'][l], wts['w_a'][l], wts['w_b'][l],
                               wts['w_c'][l], wts['w_out'][l], rw_t, rb, tiles['tm_merge'],
                               F32 if tiles['moe_routed'] else BF16)
        if tiles['moe_routed']:
            n, tm_e = bsz * t, tiles['moe_tm']
            slot, tile_expert, n_valid, fill = _moe_plan(
                gates.reshape(n, LANE), counts[..., 0].reshape(-1, N_EXPERTS), tiles['tm_merge'], tm_e)
            xs, slots = _moe_dispatch(h2.reshape(n, d), slot, fill, 2 * n + N_EXPERTS * tm_e, tiles['tm'], tm_e)
            y_rows = _moe_gemm(xs, tile_expert, n_valid, p['moe_w_gate'], p['moe_w_up'], p['moe_w_down'], l, tm_e)
            x = _moe_combine(y_rows, slots, gates, x1, mod, gfinal, final=(l == depth - 1))
        else:
            x = _moe(h2, gates, x1, mod, p['moe_w_gate'], p['moe_w_up'], p['moe_w_down'], gfinal, l,
                     tiles['moe_nb'], tiles['moe_tb'], final=(l == depth - 1))

        hs.append(s_h)
        rs.append(s5_state[:, :C_LANES].reshape(bsz, C_NGROUPS, C_STATE))
        ims.append(s5_state[:, C_LANES:].reshape(bsz, C_NGROUPS, C_STATE))
    cache5 = (depth, bsz, t, A_HEADS, LANE)
    return (x, k_all.reshape(cache5), v_all.reshape(cache5), jnp.stack(hs), jnp.stack(rs), jnp.stack(ims))


def kernel(x_prompt, x_sample, cache_k, cache_v, state_hgrn, state_s5_re, state_s5_im, c_prompt, c_sample, w_mod, b_mod, norm1_g, norm2_g, w_in, diff_lambda, diff_subln_g, hgrn_lb, hgrn_norm_g, s5_a_re, s5_a_im, s5_b_re, s5_b_im, s5_c_re, s5_c_im, s5_d, s5_log_dt, s5_w_glu, w_branch_a, w_branch_b, w_branch_c, w_out, router_w, router_b, moe_w_gate, moe_w_up, moe_w_down, final_norm_g):
    p = {
        'norm1_g': norm1_g, 'norm2_g': norm2_g, 'diff_lambda': diff_lambda, 'diff_subln_g': diff_subln_g,
        'hgrn_lb': hgrn_lb, 'hgrn_norm_g': hgrn_norm_g, 's5_a_re': s5_a_re, 's5_a_im': s5_a_im,
        's5_b_re': s5_b_re, 's5_b_im': s5_b_im, 's5_c_re': s5_c_re, 's5_c_im': s5_c_im, 's5_d': s5_d,
        's5_log_dt': s5_log_dt, 'router_w': router_w, 'router_b': router_b,
        'moe_w_gate': moe_w_gate, 'moe_w_up': moe_w_up, 'moe_w_down': moe_w_down, 'final_norm_g': final_norm_g,
    }
    wts = {'w_in': w_in.astype(BF16), 'w_glu': s5_w_glu.astype(BF16), 'w_a': w_branch_a.astype(BF16),
           'w_b': w_branch_b.astype(BF16), 'w_c': w_branch_c.astype(BF16), 'w_out': w_out.astype(BF16)}
    bp, tp, d = x_prompt.shape
    bs, ts, _ = x_sample.shape
    depth = w_mod.shape[0]
    mods = _modulation(jnp.concatenate([c_prompt, c_sample], axis=0), w_mod, b_mod)
    mods = mods.reshape(depth, bp + bs, 1, 6 * d)
    pos_p = jnp.arange(tp)
    pos_s = cache_k.shape[2] + jnp.arange(ts)
    tiles_p = dict(tm=min(256, tp), tm_merge=min(512, tp), tq=min(512, tp), tb=min(512, tp), tc=min(64, tp),
                   moe_routed=True, moe_tm=256)
    tiles_s = dict(tm=ts, tm_merge=ts, tq=ts, tb=ts, tc=ts, moe_routed=False, moe_nb=bs, moe_tb=ts)
    y_p, k_p, v_p, h_p, re_p, im_p = _trunk(x_prompt, mods[:, :bp], pos_p, p, wts, None, tiles_p)
    y_s, k_s, v_s, h_s, re_s, im_s = _trunk(x_sample, mods[:, bp:], pos_s, p, wts,
                                            (cache_k, cache_v, state_hgrn, state_s5_re, state_s5_im), tiles_s)
    return (y_p, y_s, k_p, v_p, h_p, re_p, im_p, k_s, v_s, h_s, re_s, im_s)
```
